```python
import math
import jax, jax.numpy as jnp
from jax import lax
import numpy as np

D_MODEL = 1024
BATCH = 8
SEQ = 4096
DEPTH = 1

HEAD_DIM = 64
ROPE_THETA = 10000.0
EPS = 1e-6
NEG_INF = -1e30

DIL_GROUPS = ((128, 1), (512, 4), (2048, 16))
N_DIL_GROUPS = 3
DIL_HEADS_PER_GROUP = 4
DIL_HEADS = N_DIL_GROUPS * DIL_HEADS_PER_GROUP
DIL_WIDTH = DIL_HEADS * HEAD_DIM
DIL_OUT = DIL_HEADS_PER_GROUP * HEAD_DIM

DIFF_HEADS = D_MODEL // (2 * HEAD_DIM)
DIFF_QK_WIDTH = DIFF_HEADS * 2 * HEAD_DIM
DIFF_V_WIDTH = DIFF_HEADS * 2 * HEAD_DIM
DIFF_Q_BLOCK = 128

N_BRANCHES = 2
GATE_WIDTH = N_BRANCHES * D_MODEL
IN_WIDTH = 3 * DIL_WIDTH + 2 * DIFF_QK_WIDTH + DIFF_V_WIDTH + GATE_WIDTH
IN_SPLITS = (DIL_WIDTH, 2 * DIL_WIDTH, 3 * DIL_WIDTH,
             3 * DIL_WIDTH + DIFF_QK_WIDTH,
             3 * DIL_WIDTH + 2 * DIFF_QK_WIDTH,
             3 * DIL_WIDTH + 2 * DIFF_QK_WIDTH + DIFF_V_WIDTH)

PEER_HEADS = 8
PEER_N_KEYS = 128
PEER_N_EXPERTS = PEER_N_KEYS * PEER_N_KEYS
PEER_TOPK = 16
PEER_QUERY_DIM = 256
PEER_HALF = PEER_QUERY_DIM // 2
PEER_CHUNK = 128

kernel_name = "hybrid_dilated_diffattn_peer_block"


def rms_norm(x, g):
    xf = x.astype(jnp.float32)
    y = xf * lax.rsqrt(jnp.mean(xf * xf, axis=-1, keepdims=True) + EPS)
    return (y * g.astype(jnp.float32)).astype(x.dtype)


def rope(x, positions):
    hd = x.shape[-1]
    inv_freq = 1.0 / (ROPE_THETA ** (jnp.arange(0, hd, 2, dtype=jnp.float32) / hd))
    ang = positions.astype(jnp.float32)[:, :, None, None] * inv_freq
    cos, sin = jnp.cos(ang), jnp.sin(ang)
    xf = x.astype(jnp.float32)
    x1, x2 = xf[..., : hd // 2], xf[..., hd // 2:]
    return jnp.concatenate([x1 * cos - x2 * sin, x2 * cos + x1 * sin], axis=-1).astype(x.dtype)


def dilated_window_attention(q, k, v, window, dilation):
    b, s, h, hd = q.shape
    m = window // (2 * dilation)
    blk = m
    L = s // dilation
    nb = -(-L // blk)
    Lp = nb * blk

    def to_res(t):
        return t.reshape(b, L, dilation, h, hd).transpose(0, 2, 3, 1, 4)

    qr = jnp.pad(to_res(q), ((0, 0), (0, 0), (0, 0), (0, Lp - L), (0, 0)))
    kpad = ((0, 0), (0, 0), (0, 0), (blk, Lp - L + blk), (0, 0))
    kp = jnp.pad(to_res(k), kpad)
    vp = jnp.pad(to_res(v), kpad)
    qb = qr.reshape(b, dilation, h, nb, blk, hd)

    def band(t):
        tb = t.reshape(b, dilation, h, nb + 2, blk, hd)
        return jnp.concatenate([tb[:, :, :, :-2], tb[:, :, :, 1:-1], tb[:, :, :, 2:]], axis=4)

    kb, vb = band(kp), band(vp)
    scores = jnp.einsum('brhnqc,brhnkc->brhnqk', qb, kb).astype(jnp.float32) * (hd ** -0.5)
    qi = jnp.arange(nb)[:, None, None] * blk + jnp.arange(blk)[None, :, None]
    ki = jnp.arange(nb)[:, None, None] * blk - blk + jnp.arange(3 * blk)[None, None, :]
    mask = (jnp.abs(ki - qi) <= m) & (ki >= 0) & (ki < L)
    scores = jnp.where(mask, scores, NEG_INF)
    lse = jax.nn.logsumexp(scores, axis=-1)
    p = jnp.exp(scores - lse[..., None])
    out = jnp.einsum('brhnqk,brhnkc->brhnqc', p.astype(v.dtype), vb)
    out = out.reshape(b, dilation, h, Lp, hd)[:, :, :, :L]
    out = out.transpose(0, 3, 1, 2, 4).reshape(b, s, h, hd)
    lse = lse.reshape(b, dilation, h, Lp)[..., :L].transpose(0, 3, 1, 2).reshape(b, s, h)
    return out, lse


def differential_attention(q1, q2, k1, k2, v, lam):
    b, s, h, hd = q1.shape
    nqb = s // DIFF_Q_BLOCK
    scale = hd ** -0.5

    def blockify(t):
        return t.reshape(b, nqb, DIFF_Q_BLOCK, h, hd).transpose(1, 0, 2, 3, 4)

    def one_block(qs):
        qa, qb = qs
        s1 = jnp.einsum('bqhc,bkhc->bhqk', qa, k1).astype(jnp.float32) * scale
        s2 = jnp.einsum('bqhc,bkhc->bhqk', qb, k2).astype(jnp.float32) * scale
        a = jax.nn.softmax(s1, axis=-1) - lam * jax.nn.softmax(s2, axis=-1)
        return jnp.einsum('bhqk,bkhc->bqhc', a.astype(v.dtype), v)

    out = lax.map(one_block, (blockify(q1), blockify(q2)))
    return out.transpose(1, 0, 2, 3, 4).reshape(b, s, h, 2 * hd)


def peer_ffn(h, w_query, sub_keys, expert_u, expert_v):
    b, s, d = h.shape
    tokens = h.reshape((b * s) // PEER_CHUNK, PEER_CHUNK, d)

    def chunk(xc):
        q = (xc @ w_query).reshape(PEER_CHUNK, PEER_HEADS, 2, PEER_HALF)
        s1 = jnp.einsum('thc,hnc->thn', q[:, :, 0], sub_keys[:, 0]).astype(jnp.float32)
        s2 = jnp.einsum('thc,hnc->thn', q[:, :, 1], sub_keys[:, 1]).astype(jnp.float32)
        v1, i1 = lax.top_k(s1, PEER_TOPK)
        v2, i2 = lax.top_k(s2, PEER_TOPK)
        cand = (v1[..., :, None] + v2[..., None, :]).reshape(PEER_CHUNK, PEER_HEADS, PEER_TOPK * PEER_TOPK)
        top, ci = lax.top_k(cand, PEER_TOPK)
        e1 = jnp.take_along_axis(i1, ci // PEER_TOPK, axis=-1)
        e2 = jnp.take_along_axis(i2, ci % PEER_TOPK, axis=-1)
        eidx = e1 * PEER_N_KEYS + e2
        gate = jax.nn.softmax(top, axis=-1)
        u = expert_u[eidx]
        vv = expert_v[eidx]
        act = jax.nn.gelu(jnp.einsum('td,thkd->thk', xc, u).astype(jnp.float32), approximate=False)
        wts = (gate * act).astype(xc.dtype)
        return jnp.einsum('thk,thkd->td', wts, vv)

    return lax.map(chunk, tokens).reshape(b, s, d)


def hybrid_layer(x, c, positions, lam_init, w_ada, b_ada, norm1_g, w_in, b_gate,
                 qn_a, kn_a, w_proj_a, qn_b, kn_b, lam_q1, lam_k1, lam_q2, lam_k2,
                 subln_g, w_proj_b, w_out, norm2_g, w_query, sub_keys, expert_u, expert_v):
    b, s, d = x.shape
    mod = jax.nn.silu(c) @ w_ada + b_ada
    sh1, sc1, g1, sh2, sc2, g2 = [m[:, None, :] for m in jnp.split(mod, 6, axis=-1)]

    h = rms_norm(x, norm1_g) * (1 + sc1) + sh1
    proj = h @ w_in
    qa, ka, va, qb, kb, vb, gl = jnp.split(proj, IN_SPLITS, axis=-1)

    qa = rope(rms_norm(qa.reshape(b, s, DIL_HEADS, HEAD_DIM), qn_a), positions)
    ka = rope(rms_norm(ka.reshape(b, s, DIL_HEADS, HEAD_DIM), kn_a), positions)
    va = va.reshape(b, s, DIL_HEADS, HEAD_DIM)
    outs, lses = [], []
    for gi, (win, dil) in enumerate(DIL_GROUPS):
        sl = slice(gi * DIL_HEADS_PER_GROUP, (gi + 1) * DIL_HEADS_PER_GROUP)
        o, l = dilated_window_attention(qa[:, :, sl], ka[:, :, sl], va[:, :, sl], win, dil)
        outs.append(o)
        lses.append(l)
    wg = jax.nn.softmax(jnp.stack(lses), axis=0)
    oa = jnp.sum(wg[..., None] * jnp.stack(outs).astype(jnp.float32), axis=0).astype(x.dtype)
    branch_a = oa.reshape(b, s, DIL_OUT) @ w_proj_a

    qb = rms_norm(qb.reshape(b, s, DIFF_HEADS, 2, HEAD_DIM), qn_b)
    kb = rms_norm(kb.reshape(b, s, DIFF_HEADS, 2, HEAD_DIM), kn_b)
    q1, q2 = rope(qb[:, :, :, 0], positions), rope(qb[:, :, :, 1], positions)
    k1, k2 = rope(kb[:, :, :, 0], positions), rope(kb[:, :, :, 1], positions)
    f32 = jnp.float32
    lam = (jnp.exp(jnp.sum(lam_q1.astype(f32) * lam_k1.astype(f32)))
           - jnp.exp(jnp.sum(lam_q2.astype(f32) * lam_k2.astype(f32))) + lam_init)
    ob = differential_attention(q1, q2, k1, k2, vb.reshape(b, s, DIFF_HEADS, 2 * HEAD_DIM), lam)
    ob = rms_norm(ob, subln_g) * (1.0 - lam_init)
    branch_b = ob.reshape(b, s, DIFF_V_WIDTH) @ w_proj_b

    gates = jax.nn.sigmoid((gl + b_gate).astype(f32)).astype(x.dtype).reshape(b, s, N_BRANCHES, d)
    mixed = gates[:, :, 0] * branch_a + gates[:, :, 1] * branch_b
    x = x + g1 * (mixed @ w_out)

    h2 = rms_norm(x, norm2_g) * (1 + sc2) + sh2
    x = x + g2 * peer_ffn(h2, w_query, sub_keys, expert_u, expert_v)
    return x


def setup_inputs(seed: int = 0) -> dict:
    key = jax.random.key(seed)
    ks = jax.random.split(key, 32)
    f32 = jnp.float32

    def nrm(k, shape, scale):
        return jax.random.normal(k, shape, f32) * scale

    def gain(k, shape):
        return 1.0 + 0.02 * jax.random.normal(k, shape, f32)

    L = DEPTH
    D = D_MODEL
    return {
        "x": jax.random.normal(ks[0], (BATCH, SEQ, D), f32),
        "c": jax.random.normal(ks[1], (BATCH, D), f32),
        "positions": jnp.broadcast_to(jnp.arange(SEQ, dtype=jnp.int32), (BATCH, SEQ)),
        "w_ada": nrm(ks[2], (L, D, 6 * D), 0.5 * D ** -0.5),
        "b_ada": nrm(ks[3], (L, 6 * D), 0.02),
        "norm1_g": gain(ks[4], (L, D)),
        "w_in": nrm(ks[5], (L, D, IN_WIDTH), D ** -0.5),
        "b_gate": nrm(ks[6], (L, GATE_WIDTH), 0.02),
        "qn_a": gain(ks[7], (L, HEAD_DIM)),
        "kn_a": gain(ks[8], (L, HEAD_DIM)),
        "w_proj_a": nrm(ks[9], (L, DIL_OUT, D), DIL_OUT ** -0.5),
        "qn_b": gain(ks[10], (L, HEAD_DIM)),
        "kn_b": gain(ks[11], (L, HEAD_DIM)),
        "lam_q1": nrm(ks[12], (L, HEAD_DIM), 0.1),
        "lam_k1": nrm(ks[13], (L, HEAD_DIM), 0.1),
        "lam_q2": nrm(ks[14], (L, HEAD_DIM), 0.1),
        "lam_k2": nrm(ks[15], (L, HEAD_DIM), 0.1),
        "subln_g": gain(ks[16], (L, 2 * HEAD_DIM)),
        "w_proj_b": nrm(ks[17], (L, DIFF_V_WIDTH, D), DIFF_V_WIDTH ** -0.5),
        "w_out": nrm(ks[18], (L, D, D), D ** -0.5),
        "norm2_g": gain(ks[19], (L, D)),
        "w_query": nrm(ks[20], (L, D, PEER_HEADS * PEER_QUERY_DIM), D ** -0.5),
        "sub_keys": nrm(ks[21], (L, PEER_HEADS, 2, PEER_N_KEYS, PEER_HALF), PEER_HALF ** -0.5),
        "expert_u": nrm(ks[22], (L, PEER_N_EXPERTS, D), D ** -0.5),
        "expert_v": nrm(ks[23], (L, PEER_N_EXPERTS, D), PEER_HEADS ** -0.5),
    }


def reference(x, c, positions, w_ada, b_ada, norm1_g, w_in, b_gate, qn_a, kn_a, w_proj_a,
              qn_b, kn_b, lam_q1, lam_k1, lam_q2, lam_k2, subln_g, w_proj_b, w_out,
              norm2_g, w_query, sub_keys, expert_u, expert_v):
    for l in range(DEPTH):
        lam_init = 0.8 - 0.6 * math.exp(-0.3 * l)
        x = hybrid_layer(x, c, positions, lam_init, w_ada[l], b_ada[l], norm1_g[l], w_in[l],
                         b_gate[l], qn_a[l], kn_a[l], w_proj_a[l], qn_b[l], kn_b[l],
                         lam_q1[l], lam_k1[l], lam_q2[l], lam_k2[l], subln_g[l], w_proj_b[l],
                         w_out[l], norm2_g[l], w_query[l], sub_keys[l], expert_u[l], expert_v[l])
    return x
```

```python
import functools
import math

import jax
import jax.numpy as jnp
from jax import lax
from jax.experimental import pallas as pl
from jax.experimental.pallas import tpu as pltpu

F32 = jnp.float32
BF16 = jnp.bfloat16

HEAD_DIM = 64
ROPE_THETA = 10000.0
EPS = 1e-6
NEG_INF = -1e30

DIL_GROUPS = ((128, 1), (512, 4), (2048, 16))
DIL_HEADS_PER_GROUP = 4
DIL_GROUP_WIDTH = DIL_HEADS_PER_GROUP * HEAD_DIM
DIL_SIDE = 64

DIFF_HEADS = 8
PEER_HEADS = 8
PEER_N_KEYS = 128
PEER_TOPK = 16
PEER_HALF = 128

LANES = 128
VMEM_LIMIT = 56 * 1024 * 1024

_NT = (((1,), (1,)), ((), ()))


def _cparams(n_axes):
    return pltpu.CompilerParams(dimension_semantics=("arbitrary",) * n_axes,
                                vmem_limit_bytes=VMEM_LIMIT)


def _ada_kernel(c_ref, w_ref, b_ref, o_ref):
    c = c_ref[...]
    sc = (c * jax.nn.sigmoid(c)).astype(BF16)
    o_ref[...] = jnp.dot(sc, w_ref[...].astype(BF16), preferred_element_type=F32) + b_ref[...]


def _ada(c, w_ada, b_ada):
    bsz, d = c.shape
    n = w_ada.shape[1]
    tn = 1024
    return pl.pallas_call(
        _ada_kernel,
        out_shape=jax.ShapeDtypeStruct((bsz, n), F32),
        grid=(n // tn,),
        in_specs=[pl.BlockSpec((bsz, d), lambda j: (0, 0)),
                  pl.BlockSpec((d, tn), lambda j: (0, j)),
                  pl.BlockSpec((1, tn), lambda j: (0, j))],
        out_specs=pl.BlockSpec((bsz, tn), lambda j: (0, j)),
        compiler_params=_cparams(1),
        name="ada_mod",
    )(c, w_ada, b_ada.reshape(1, n))


def _rope_kernel(pos_ref, freq_ref, cos_ref, sin_ref):
    ang = pos_ref[...] * freq_ref[...]
    lane = lax.broadcasted_iota(jnp.int32, ang.shape, 1)
    first_half = (lane % HEAD_DIM) < (HEAD_DIM // 2)
    cos_ref[...] = jnp.cos(ang)
    s = jnp.sin(ang)
    sin_ref[...] = jnp.where(first_half, -s, s)


def _rope_tables(positions):
    t = positions.size
    tm = min(1024, t)
    pos = positions.reshape(t, 1).astype(F32)
    inv_freq = 1.0 / (ROPE_THETA ** (jnp.arange(0, HEAD_DIM, 2, dtype=F32) / HEAD_DIM))
    freq = jnp.tile(inv_freq, LANES // (HEAD_DIM // 2)).reshape(1, LANES)
    return pl.pallas_call(
        _rope_kernel,
        out_shape=(jax.ShapeDtypeStruct((t, LANES), F32), jax.ShapeDtypeStruct((t, LANES), F32)),
        grid=(t // tm,),
        in_specs=[pl.BlockSpec((tm, 1), lambda i: (i, 0)),
                  pl.BlockSpec((1, LANES), lambda i: (0, 0))],
        out_specs=(pl.BlockSpec((tm, LANES), lambda i: (i, 0)),
                   pl.BlockSpec((tm, LANES), lambda i: (i, 0))),
        compiler_params=_cparams(1),
        name="rope_tables",
    )(pos, freq)


def _modulated_norm(x, g, scale, shift):
    ms = jnp.mean(x * x, axis=-1, keepdims=True)
    return (x * lax.rsqrt(ms + EPS)) * g * (1.0 + scale) + shift


def _head_norm_rope(acc, gain, cos, sin, gmat):
    ss = jnp.dot((acc * acc).astype(BF16), gmat, preferred_element_type=F32)
    yn = acc * lax.rsqrt(ss * (1.0 / HEAD_DIM) + EPS) * gain
    lane = lax.broadcasted_iota(jnp.int32, cos.shape, 1)
    first_half = (lane % HEAD_DIM) < (HEAD_DIM // 2)
    outs = []
    for hf in range(acc.shape[1] // LANES):
        y = yn[:, hf * LANES:(hf + 1) * LANES]
        up = pltpu.roll(y, HEAD_DIM // 2, axis=1)
        down = pltpu.roll(y, LANES - HEAD_DIM // 2, axis=1)
        swapped = jnp.where(first_half, down, up)
        outs.append(y * cos + swapped * sin)
    return jnp.concatenate(outs, axis=1)


def _group_sum_matrix():
    r = lax.broadcasted_iota(jnp.int32, (2 * LANES, 2 * LANES), 0) // HEAD_DIM
    c = lax.broadcasted_iota(jnp.int32, (2 * LANES, 2 * LANES), 1) // HEAD_DIM
    return (r == c).astype(BF16)


def _inproj_a_kernel(x_ref, mod_ref, ng_ref, w_ref, gain_ref, cos_ref, sin_ref, gmat_ref,
                     o0_ref, o1_ref, o2_ref, h_sc, y_sc):
    j = pl.program_id(1)
    tm = x_ref.shape[0]

    @pl.when(j == 0)
    def _():
        m = mod_ref[0]
        h_sc[...] = _modulated_norm(x_ref[...], ng_ref[...], m[1:2], m[0:1]).astype(BF16)

    acc = jnp.dot(h_sc[...], w_ref[...], preferred_element_type=F32)
    outs = (o0_ref, o1_ref, o2_ref)

    def emit(get_piece):
        for g, (_, dil) in enumerate(DIL_GROUPS):
            y = get_piece(g)
            if dil == 1:
                outs[g][0, 0, 0] = y.astype(BF16)
            else:
                rows = tm // dil
                for hf in range(DIL_GROUP_WIDTH // LANES):
                    y_sc[hf] = y[:, hf * LANES:(hf + 1) * LANES]
                for r in range(dil):
                    piece = jnp.concatenate(
                        [y_sc[hf, pl.ds(r, rows, stride=dil), :] for hf in range(DIL_GROUP_WIDTH // LANES)],
                        axis=1)
                    outs[g][0, 0, r] = piece.astype(BF16)

    @pl.when(j < 2)
    def _():
        cos = cos_ref[...]
        sin = sin_ref[...]
        gmat = gmat_ref[...]
        emit(lambda g: _head_norm_rope(acc[:, g * DIL_GROUP_WIDTH:(g + 1) * DIL_GROUP_WIDTH],
                                       gain_ref[:, g * DIL_GROUP_WIDTH:(g + 1) * DIL_GROUP_WIDTH],
                                       cos, sin, gmat))

    @pl.when(j == 2)
    def _():
        emit(lambda g: acc[:, g * DIL_GROUP_WIDTH:(g + 1) * DIL_GROUP_WIDTH])


def _inproj_a(x2, mod3, norm_g, w_a, gain_a, cos, sin, gmat, bsz, seq):
    t, d = x2.shape
    tm = min(1024, seq)
    spt = seq // tm
    n = w_a.shape[1]
    tn = n // 3
    out_shapes, out_specs = [], []
    for _, dil in DIL_GROUPS:
        out_shapes.append(jax.ShapeDtypeStruct((3, bsz, dil, seq // dil, DIL_GROUP_WIDTH), BF16))
        out_specs.append(pl.BlockSpec((1, 1, dil, tm // dil, DIL_GROUP_WIDTH),
                                      lambda i, j: (j, i // spt, 0, i % spt, 0)))
    return pl.pallas_call(
        _inproj_a_kernel,
        out_shape=tuple(out_shapes),
        grid=(t // tm, 3),
        in_specs=[pl.BlockSpec((tm, d), lambda i, j: (i, 0)),
                  pl.BlockSpec((1, 6, d), lambda i, j: (i // spt, 0, 0)),
                  pl.BlockSpec((1, d), lambda i, j: (0, 0)),
                  pl.BlockSpec((d, tn), lambda i, j: (0, j)),
                  pl.BlockSpec((1, tn), lambda i, j: (0, j)),
                  pl.BlockSpec((tm, LANES), lambda i, j: (i, 0)),
                  pl.BlockSpec((tm, LANES), lambda i, j: (i, 0)),
                  pl.BlockSpec((2 * LANES, 2 * LANES), lambda i, j: (0, 0))],
        out_specs=tuple(out_specs),
        scratch_shapes=[pltpu.VMEM((tm, d), BF16),
                        pltpu.VMEM((DIL_GROUP_WIDTH // LANES, tm, LANES), F32)],
        compiler_params=_cparams(2),
        name="inproj_dilated",
    )(x2, mod3, norm_g, w_a, gain_a, cos, sin, gmat)


def _inproj_b_kernel(x_ref, mod_ref, ng_ref, w_ref, gain_ref, bias_ref, cos_ref, sin_ref, gmat_ref,
                     o_ref, h_sc, *, n_rope_tiles, n_plain_tiles):
    j = pl.program_id(1)

    @pl.when(j == 0)
    def _():
        m = mod_ref[0]
        h_sc[...] = _modulated_norm(x_ref[...], ng_ref[...], m[1:2], m[0:1]).astype(BF16)

    acc = jnp.dot(h_sc[...], w_ref[...], preferred_element_type=F32)
    tn = acc.shape[1]

    @pl.when(j < n_rope_tiles)
    def _():
        cos = cos_ref[...]
        sin = sin_ref[...]
        gmat = gmat_ref[...]
        for p in range(tn // (2 * LANES)):
            sl = slice(p * 2 * LANES, (p + 1) * 2 * LANES)
            o_ref[:, sl] = _head_norm_rope(acc[:, sl], gain_ref[:, sl], cos, sin, gmat).astype(BF16)

    @pl.when((j >= n_rope_tiles) & (j < n_rope_tiles + n_plain_tiles))
    def _():
        o_ref[...] = acc.astype(BF16)

    @pl.when(j >= n_rope_tiles + n_plain_tiles)
    def _():
        o_ref[...] = jax.nn.sigmoid(acc + bias_ref[...]).astype(BF16)


def _inproj_b(x2, mod3, norm_g, w_b, gain_b, bias_b, cos, sin, gmat, seq, qk_width, v_width):
    t, d = x2.shape
    tm = min(1024, seq)
    spt = seq // tm
    n = w_b.shape[1]
    tn = 512
    kern = functools.partial(_inproj_b_kernel, n_rope_tiles=2 * qk_width // tn, n_plain_tiles=v_width // tn)
    return pl.pallas_call(
        kern,
        out_shape=jax.ShapeDtypeStruct((t, n), BF16),
        grid=(t // tm, n // tn),
        in_specs=[pl.BlockSpec((tm, d), lambda i, j: (i, 0)),
                  pl.BlockSpec((1, 6, d), lambda i, j: (i // spt, 0, 0)),
                  pl.BlockSpec((1, d), lambda i, j: (0, 0)),
                  pl.BlockSpec((d, tn), lambda i, j: (0, j)),
                  pl.BlockSpec((1, tn), lambda i, j: (0, j)),
                  pl.BlockSpec((1, tn), lambda i, j: (0, j)),
                  pl.BlockSpec((tm, LANES), lambda i, j: (i, 0)),
                  pl.BlockSpec((tm, LANES), lambda i, j: (i, 0)),
                  pl.BlockSpec((2 * LANES, 2 * LANES), lambda i, j: (0, 0))],
        out_specs=pl.BlockSpec((tm, tn), lambda i, j: (i, j)),
        scratch_shapes=[pltpu.VMEM((tm, d), BF16)],
        compiler_params=_cparams(2),
        name="inproj_diff_gates",
    )(x2, mod3, norm_g, w_b, gain_b, bias_b, cos, sin, gmat)


def _dilated_attn_kernel(q_ref, k_ref, v_ref, o_ref, lse_ref, *, tq, kw, length):
    qi = pl.program_id(1)
    q0 = qi * tq
    kstart = jnp.clip(q0 - DIL_SIDE, 0, length - kw)
    kstart = pl.multiple_of(kstart, DIL_SIDE)
    q = q_ref[0, 0]
    k = k_ref[0, 0, pl.ds(kstart, kw), :]
    v = v_ref[0, 0, pl.ds(kstart, kw), :]
    lane_head = lax.broadcasted_iota(jnp.int32, (tq, DIL_GROUP_WIDTH), 1) // HEAD_DIM
    qidx = q0 + lax.broadcasted_iota(jnp.int32, (tq, kw), 0)
    kidx = kstart + lax.broadcasted_iota(jnp.int32, (tq, kw), 1)
    valid = jnp.abs(kidx - qidx) <= DIL_SIDE
    out = jnp.zeros((tq, DIL_GROUP_WIDTH), F32)
    lse_full = jnp.zeros((tq, DIL_GROUP_WIDTH), F32)
    for h in range(DIL_HEADS_PER_GROUP):
        in_head = lane_head == h
        qh = jnp.where(in_head, q, jnp.zeros_like(q))
        s = lax.dot_general(qh, k, _NT, preferred_element_type=F32)
        s = jnp.where(valid, s, NEG_INF)
        m = jnp.max(s, axis=-1, keepdims=True)
        p = jnp.exp(s - m)
        l = jnp.sum(p, axis=-1, keepdims=True)
        oh = jnp.dot(p.astype(BF16), v, preferred_element_type=F32)
        out = jnp.where(in_head, oh / l, out)
        lse_full = jnp.where(in_head, m + jnp.log(l), lse_full)
    o_ref[0] = out.astype(BF16)
    lse_ref[0] = lse_full


def _dilated_attn(qkv):
    _, bsz, dil, length, w = qkv.shape
    bd = bsz * dil
    qkv = qkv.reshape(3, bd, length, w)
    tq = min(128, length)
    kw = min(tq + 2 * DIL_SIDE, length)
    kern = functools.partial(_dilated_attn_kernel, tq=tq, kw=kw, length=length)
    return pl.pallas_call(
        kern,
        out_shape=(jax.ShapeDtypeStruct((bd, length, w), BF16), jax.ShapeDtypeStruct((bd, length, w), F32)),
        grid=(bd, length // tq),
        in_specs=[pl.BlockSpec((1, 1, tq, w), lambda b, i: (0, b, i, 0)),
                  pl.BlockSpec((1, 1, length, w), lambda b, i: (1, b, 0, 0)),
                  pl.BlockSpec((1, 1, length, w), lambda b, i: (2, b, 0, 0))],
        out_specs=(pl.BlockSpec((1, tq, w), lambda b, i: (b, i, 0)),
                   pl.BlockSpec((1, tq, w), lambda b, i: (b, i, 0))),
        compiler_params=_cparams(2),
        name="dilated_attn",
    )(qkv, qkv, qkv)


def _dil_combine_kernel(o0_ref, l0_ref, o1_ref, l1_ref, o2_ref, l2_ref, oa_ref, nat_sc):
    tm = oa_ref.shape[0]
    n_half = DIL_GROUP_WIDTH // LANES

    def natural(ref, dil):
        if dil == 1:
            return ref[0, 0].astype(F32)
        rows = tm // dil
        for r in range(dil):
            blk = ref[0, r].astype(F32)
            for hf in range(n_half):
                nat_sc[hf, pl.ds(r, rows, stride=dil), :] = blk[:, hf * LANES:(hf + 1) * LANES]
        return jnp.concatenate([nat_sc[hf] for hf in range(n_half)], axis=1)

    dils = [d for _, d in DIL_GROUPS]
    lses = [natural(r, d) for r, d in zip((l0_ref, l1_ref, l2_ref), dils)]
    m = jnp.maximum(jnp.maximum(lses[0], lses[1]), lses[2])
    ws = [jnp.exp(l - m) for l in lses]
    den = ws[0] + ws[1] + ws[2]
    num = jnp.zeros_like(den)
    for w, r, d in zip(ws, (o0_ref, o1_ref, o2_ref), dils):
        num = num + w * natural(r, d)
    oa_ref[...] = (num / den).astype(BF16)


def _dil_combine(outs, lses, bsz, seq):
    tm = min(512, seq)
    spt = seq // tm
    in_specs, args = [], []
    for (o, l), (_, dil) in zip(zip(outs, lses), DIL_GROUPS):
        shape4 = (bsz, dil, seq // dil, DIL_GROUP_WIDTH)
        spec = pl.BlockSpec((1, dil, tm // dil, DIL_GROUP_WIDTH), lambda i: (i // spt, 0, i % spt, 0))
        in_specs += [spec, spec]
        args += [o.reshape(shape4), l.reshape(shape4)]
    return pl.pallas_call(
        _dil_combine_kernel,
        out_shape=jax.ShapeDtypeStruct((bsz * seq, DIL_GROUP_WIDTH), BF16),
        grid=(bsz * spt,),
        in_specs=in_specs,
        out_specs=pl.BlockSpec((tm, DIL_GROUP_WIDTH), lambda i: (i, 0)),
        scratch_shapes=[pltpu.VMEM((DIL_GROUP_WIDTH // LANES, tm, LANES), F32)],
        compiler_params=_cparams(1),
        name="dilated_combine",
    )(*args)


def _diff_attn_kernel(q_ref, k_ref, v_ref, lq1_ref, lk1_ref, lq2_ref, lk2_ref, sg_ref, o_ref,
                      *, tk, lam_init):
    tq = q_ref.shape[0]
    seq = k_ref.shape[0]
    q = q_ref[...]
    lane = lax.broadcasted_iota(jnp.int32, q.shape, 1)
    zero = jnp.zeros_like(q)
    qz = jnp.concatenate([jnp.where(lane < HEAD_DIM, q, zero), jnp.where(lane >= HEAD_DIM, q, zero)], axis=0)

    def body(c, carry):
        m, l, acc = carry
        ks = pl.multiple_of(c * tk, tk)
        k = k_ref[pl.ds(ks, tk), :]
        v = v_ref[pl.ds(ks, tk), :]
        s = lax.dot_general(qz, k, _NT, preferred_element_type=F32)
        m_new = jnp.maximum(m, jnp.max(s, axis=-1, keepdims=True))
        alpha = jnp.exp(m - m_new)
        p = jnp.exp(s - m_new)
        l = alpha * l + jnp.sum(p, axis=-1, keepdims=True)
        acc = alpha * acc + jnp.dot(p.astype(BF16), v, preferred_element_type=F32)
        return m_new, l, acc

    init = (jnp.full((2 * tq, 1), -jnp.inf, F32), jnp.zeros((2 * tq, 1), F32),
            jnp.zeros((2 * tq, v_ref.shape[1]), F32))
    _, l, acc = lax.fori_loop(0, seq // tk, body, init)
    o = acc / l
    lam = (jnp.exp(jnp.sum(lq1_ref[...] * lk1_ref[...], axis=-1, keepdims=True))
           - jnp.exp(jnp.sum(lq2_ref[...] * lk2_ref[...], axis=-1, keepdims=True)) + lam_init)
    ob = o[:tq] - lam * o[tq:]
    ms = jnp.mean(ob * ob, axis=-1, keepdims=True)
    o_ref[...] = (ob * lax.rsqrt(ms + EPS) * sg_ref[...] * (1.0 - lam_init)).astype(BF16)


def _diff_attn(proj_b, lam_vecs, subln_g, bsz, seq, lam_init):
    t = proj_b.shape[0]
    tq = min(256, seq)
    tk = min(512, seq)
    qpt = seq // tq
    hw = 2 * HEAD_DIM
    kern = functools.partial(_diff_attn_kernel, tk=tk, lam_init=lam_init)
    vec_spec = pl.BlockSpec((1, HEAD_DIM), lambda b, h, i: (0, 0))
    return pl.pallas_call(
        kern,
        out_shape=jax.ShapeDtypeStruct((t, DIFF_HEADS * hw), BF16),
        grid=(bsz, DIFF_HEADS, qpt),
        in_specs=[pl.BlockSpec((tq, hw), lambda b, h, i: (b * qpt + i, h)),
                  pl.BlockSpec((seq, hw), lambda b, h, i: (b, DIFF_HEADS + h)),
                  pl.BlockSpec((seq, hw), lambda b, h, i: (b, 2 * DIFF_HEADS + h)),
                  vec_spec, vec_spec, vec_spec, vec_spec,
                  pl.BlockSpec((1, hw), lambda b, h, i: (0, 0))],
        out_specs=pl.BlockSpec((tq, hw), lambda b, h, i: (b * qpt + i, h)),
        compiler_params=_cparams(3),
        name="diff_attn",
    )(proj_b, proj_b, proj_b, *lam_vecs, subln_g)


def _merge_kernel(oa_ref, ob_ref, ga_ref, gb_ref, x_ref, mod_ref, ng_ref, wpa_ref, wpb_ref, wo_ref,
                  x1_ref, h2_ref):
    ba = jnp.dot(oa_ref[...], wpa_ref[...], preferred_element_type=F32)
    bb = jnp.dot(ob_ref[...], wpb_ref[...], preferred_element_type=F32)
    mixed = ga_ref[...].astype(F32) * ba + gb_ref[...].astype(F32) * bb
    mo = jnp.dot(mixed.astype(BF16), wo_ref[...], preferred_element_type=F32)
    m = mod_ref[0]
    x1 = x_ref[...] + m[2:3] * mo
    x1_ref[...] = x1
    h2_ref[...] = _modulated_norm(x1, ng_ref[...], m[4:5], m[3:4]).astype(BF16)


def _merge(oa, ob, proj_b, x2, mod3, norm2_g, wpa, wpb, wo, seq, gate_block0):
    t, d = x2.shape
    tm = min(512, seq)
    spt = seq // tm
    full = lambda i: (0, 0)
    return pl.pallas_call(
        _merge_kernel,
        out_shape=(jax.ShapeDtypeStruct((t, d), F32), jax.ShapeDtypeStruct((t, d), BF16)),
        grid=(t // tm,),
        in_specs=[pl.BlockSpec((tm, oa.shape[1]), lambda i: (i, 0)),
                  pl.BlockSpec((tm, d), lambda i: (i, 0)),
                  pl.BlockSpec((tm, d), lambda i: (i, gate_block0)),
                  pl.BlockSpec((tm, d), lambda i: (i, gate_block0 + 1)),
                  pl.BlockSpec((tm, d), lambda i: (i, 0)),
                  pl.BlockSpec((1, 6, d), lambda i: (i // spt, 0, 0)),
                  pl.BlockSpec((1, d), full),
                  pl.BlockSpec(wpa.shape, full),
                  pl.BlockSpec(wpb.shape, full),
                  pl.BlockSpec(wo.shape, full)],
        out_specs=(pl.BlockSpec((tm, d), lambda i: (i, 0)), pl.BlockSpec((tm, d), lambda i: (i, 0))),
        compiler_params=_cparams(1),
        name="merge_outproj_norm2",
    )(oa, ob, proj_b, proj_b, x2, mod3, norm2_g, wpa, wpb, wo)


def _extract_topk(s, idx_f, val_sc, idx_sc):
    for k in range(PEER_TOPK):
        m = jnp.max(s, axis=0, keepdims=True)
        idx = jnp.min(jnp.where(s == m, idx_f, 1e9), axis=0, keepdims=True)
        val_sc[k:k + 1, :] = m
        idx_sc[k:k + 1, :] = idx
        s = jnp.where(idx_f == idx, -jnp.inf, s)


def _peer_topk_kernel(h_ref, wq_ref, sk_ref, e1_ref, e2_ref, g_ref,
                      q_sc, v1_sc, i1_sc, v2_sc, i2_sc, top_sc, lab_sc, e1t_sc, e2t_sc, gt_sc):
    tm = h_ref.shape[0]
    q = jnp.dot(h_ref[...], wq_ref[...], preferred_element_type=F32).astype(BF16)
    for piece in range(2 * PEER_HEADS):
        q_sc[piece] = q[:, piece * PEER_HALF:(piece + 1) * PEER_HALF]
    key_idx = lax.broadcasted_iota(jnp.int32, (PEER_N_KEYS, tm), 0).astype(F32)
    sub8 = lax.broadcasted_iota(jnp.int32, (8, tm), 0).astype(F32)
    neg = jnp.full((8, tm), -jnp.inf, F32)

    def head(h, carry):
        for side, (v_sc, i_sc) in enumerate(((v1_sc, i1_sc), (v2_sc, i2_sc))):
            qs = q_sc[2 * h + side]
            st = lax.dot_general(sk_ref[2 * h + side], qs, _NT, preferred_element_type=F32)
            _extract_topk(st, key_idx, v_sc, i_sc)
        v2lo = v2_sc[0:8, :]
        cands = [v1_sc[0:1, :] + v2lo, v1_sc[0:1, :] + v2_sc[8:16, :]]
        labels = [sub8, sub8 + 8.0]
        for a in range(1, 8):
            nb = PEER_TOPK // (a + 1)
            cands.append(jnp.where(sub8 < nb, v1_sc[a:a + 1, :] + v2lo, neg))
            labels.append(sub8 + float(a * PEER_TOPK))
        cands.append(v1_sc[8:16, :] + v2_sc[0:1, :])
        labels.append((sub8 + 8.0) * float(PEER_TOPK))
        cand = jnp.concatenate(cands, axis=0)
        label = jnp.concatenate(labels, axis=0)
        _extract_topk(cand, label, top_sc, lab_sc)
        top = top_sc[...]
        lab = lab_sc[...]
        a_sel = jnp.floor(lab * (1.0 / PEER_TOPK))
        b_sel = lab - a_sel * PEER_TOPK
        e1 = jnp.zeros_like(top)
        e2 = jnp.zeros_like(top)
        for r in range(PEER_TOPK):
            e1 = jnp.where(a_sel == float(r), i1_sc[r:r + 1, :], e1)
            e2 = jnp.where(b_sel == float(r), i2_sc[r:r + 1, :], e2)
        p = jnp.exp(top - jnp.max(top, axis=0, keepdims=True))
        gate = p / jnp.sum(p, axis=0, keepdims=True)
        row = pl.multiple_of(h * PEER_TOPK, PEER_TOPK)
        e1t_sc[pl.ds(row, PEER_TOPK), :] = e1
        e2t_sc[pl.ds(row, PEER_TOPK), :] = e2
        gt_sc[pl.ds(row, PEER_TOPK), :] = gate
        return carry

    lax.fori_loop(0, PEER_HEADS, head, 0)
    e1_ref[...] = e1t_sc[...].T
    e2_ref[...] = e2t_sc[...].T
    g_ref[...] = gt_sc[...].T


def _peer_topk(h2, wq, sk):
    t, d = h2.shape
    tm = min(256, t)
    nsel = PEER_HEADS * PEER_TOPK
    out = jax.ShapeDtypeStruct((t, nsel), F32)
    spec = pl.BlockSpec((tm, nsel), lambda i: (i, 0))
    small = lambda: pltpu.VMEM((PEER_TOPK, tm), F32)
    big = lambda: pltpu.VMEM((nsel, tm), F32)
    return pl.pallas_call(
        _peer_topk_kernel,
        out_shape=(out, out, out),
        grid=(t // tm,),
        in_specs=[pl.BlockSpec((tm, d), lambda i: (i, 0)),
                  pl.BlockSpec(wq.shape, lambda i: (0, 0)),
                  pl.BlockSpec(sk.shape, lambda i: (0, 0, 0))],
        out_specs=(spec, spec, spec),
        scratch_shapes=[pltpu.VMEM((2 * PEER_HEADS, tm, PEER_HALF), BF16),
                        small(), small(), small(), small(), small(), small(), big(), big(), big()],
        compiler_params=_cparams(1),
        name="peer_topk",
    )(h2, wq, sk)


def _peer_gates_kernel(e1_ref, e2_ref, g_ref, o_ref, g_sc, *, pitch):
    tm = e1_ref.shape[0]
    row_idx = lax.broadcasted_iota(jnp.int32, (PEER_N_KEYS, e1_ref.shape[1]), 0).astype(F32)

    def token(t, carry):
        r1 = e1_ref[pl.ds(t, 1), :]
        r2 = e2_ref[pl.ds(t, 1), :]
        gt = g_ref[pl.ds(t, 1), :]
        sel_i = jnp.where(row_idx == r1, 1.0, 0.0).astype(BF16)
        sel_j = jnp.where(row_idx == r2, gt, 0.0).astype(BF16)
        gm = lax.dot_general(sel_i, sel_j, _NT, preferred_element_type=F32)
        g_sc[pl.ds(t, PEER_N_KEYS, stride=pitch), :] = gm
        return carry

    lax.fori_loop(0, tm, token, 0)

    def emit(i, carry):
        start = pl.multiple_of(i * pitch, 8)
        o_ref[i] = g_sc[pl.ds(start, tm), :].astype(BF16)
        return carry

    lax.fori_loop(0, PEER_N_KEYS, emit, 0)


def _peer_gates(e1, e2, gate):
    t, nsel = e1.shape
    tm = min(128, t)
    pitch = tm + 8
    spec = pl.BlockSpec((tm, nsel), lambda i: (i, 0))
    return pl.pallas_call(
        functools.partial(_peer_gates_kernel, pitch=pitch),
        out_shape=jax.ShapeDtypeStruct((PEER_N_KEYS, t, PEER_N_KEYS), BF16),
        grid=(t // tm,),
        in_specs=[spec, spec, spec],
        out_specs=pl.BlockSpec((PEER_N_KEYS, tm, PEER_N_KEYS), lambda i: (0, i, 0)),
        scratch_shapes=[pltpu.VMEM((PEER_N_KEYS * pitch, PEER_N_KEYS), F32)],
        compiler_params=_cparams(1),
        name="peer_gate_matrix",
    )(e1, e2, gate)


def _peer_dense_kernel(h_ref, u_ref, v_ref, g_ref, x1_ref, mod_ref, o_ref, acc_sc):
    c = pl.program_id(1)

    @pl.when(c == 0)
    def _():
        acc_sc[...] = jnp.zeros_like(acc_sc)

    a = lax.dot_general(h_ref[...], u_ref[...], _NT, preferred_element_type=F32)
    act = 0.5 * a * (1.0 + lax.erf(a * (1.0 / math.sqrt(2.0))))
    ws = [(act[:, ib * LANES:(ib + 1) * LANES] * g_ref[ib].astype(F32)).astype(BF16)
          for ib in range(g_ref.shape[0])]
    w = jnp.concatenate(ws, axis=1)
    acc_sc[...] += jnp.dot(w, v_ref[...], preferred_element_type=F32)

    @pl.when(c == pl.num_programs(1) - 1)
    def _():
        o_ref[...] = x1_ref[...] + mod_ref[0][5:6] * acc_sc[...]


def _peer_dense(h2, u, v, gmat, x1, mod3, seq):
    t, d = h2.shape
    n_exp = u.shape[0]
    tm = min(1024, seq)
    spt = seq // tm
    tn = 512
    return pl.pallas_call(
        _peer_dense_kernel,
        out_shape=jax.ShapeDtypeStruct((t, d), F32),
        grid=(t // tm, n_exp // tn),
        in_specs=[pl.BlockSpec((tm, d), lambda i, c: (i, 0)),
                  pl.BlockSpec((tn, d), lambda i, c: (c, 0)),
                  pl.BlockSpec((tn, d), lambda i, c: (c, 0)),
                  pl.BlockSpec((tn // PEER_N_KEYS, tm, PEER_N_KEYS), lambda i, c: (c, i, 0)),
                  pl.BlockSpec((tm, d), lambda i, c: (i, 0)),
                  pl.BlockSpec((1, 6, d), lambda i, c: (i // spt, 0, 0))],
        out_specs=pl.BlockSpec((tm, d), lambda i, c: (i, 0)),
        scratch_shapes=[pltpu.VMEM((tm, d), F32)],
        compiler_params=_cparams(2),
        name="peer_dense",
    )(h2, u, v, gmat, x1, mod3)


def _layer(x, c, cos, sin, lam_init, w_ada, b_ada, norm1_g, w_in, b_gate, qn_a, kn_a, w_proj_a,
           qn_b, kn_b, lam_q1, lam_k1, lam_q2, lam_k2, subln_g, w_proj_b, w_out, norm2_g,
           w_query, sub_keys, expert_u, expert_v):
    bsz, seq, d = x.shape
    t = bsz * seq
    x2 = x.reshape(t, d)
    mod3 = _ada(c, w_ada, b_ada).reshape(bsz, 6, d)
    gmat = _group_sum_matrix()
    scale = HEAD_DIM ** -0.5

    dil_w = len(DIL_GROUPS) * DIL_GROUP_WIDTH
    diff_w = DIFF_HEADS * 2 * HEAD_DIM
    a_cols = 3 * dil_w
    w_in_bf = w_in.astype(BF16)
    ones = lambda n: jnp.ones((n,), F32)
    gain_a = jnp.concatenate([jnp.tile(qn_a, dil_w // HEAD_DIM) * scale, jnp.tile(kn_a, dil_w // HEAD_DIM),
                              ones(dil_w)]).reshape(1, a_cols)
    qkv_groups = _inproj_a(x2, mod3, norm1_g.reshape(1, d), w_in_bf[:, :a_cols], gain_a, cos, sin, gmat, bsz, seq)

    n_b = w_in.shape[1] - a_cols
    gain_b = jnp.concatenate([jnp.tile(qn_b, diff_w // HEAD_DIM) * scale, jnp.tile(kn_b, diff_w // HEAD_DIM),
                              ones(n_b - 2 * diff_w)]).reshape(1, n_b)
    bias_b = jnp.concatenate([jnp.zeros((3 * diff_w,), F32), b_gate]).reshape(1, n_b)
    proj_b = _inproj_b(x2, mod3, norm1_g.reshape(1, d), w_in_bf[:, a_cols:], gain_b, bias_b, cos, sin, gmat,
                       seq, diff_w, diff_w)

    outs, lses = [], []
    for qkv in qkv_groups:
        o, l = _dilated_attn(qkv)
        outs.append(o)
        lses.append(l)
    oa = _dil_combine(outs, lses, bsz, seq)

    lam_vecs = [v.reshape(1, HEAD_DIM) for v in (lam_q1, lam_k1, lam_q2, lam_k2)]
    ob = _diff_attn(proj_b, lam_vecs, subln_g.reshape(1, 2 * HEAD_DIM), bsz, seq, lam_init)

    x1, h2 = _merge(oa, ob, proj_b, x2, mod3, norm2_g.reshape(1, d), w_proj_a.astype(BF16),
                    w_proj_b.astype(BF16), w_out.astype(BF16), seq, gate_block0=3 * diff_w // d)

    sk = sub_keys.astype(BF16).reshape(PEER_HEADS * 2, PEER_N_KEYS, PEER_HALF)
    e1, e2, gate = _peer_topk(h2, w_query.astype(BF16), sk)
    gdense = _peer_gates(e1, e2, gate)
    out = _peer_dense(h2, expert_u.astype(BF16), expert_v.astype(BF16), gdense, x1, mod3, seq)
    return out.reshape(bsz, seq, d)


def kernel(x, c, positions, w_ada, b_ada, norm1_g, w_in, b_gate, qn_a, kn_a, w_proj_a, qn_b, kn_b,
           lam_q1, lam_k1, lam_q2, lam_k2, subln_g, w_proj_b, w_out, norm2_g, w_query, sub_keys,
           expert_u, expert_v):
    cos, sin = _rope_tables(positions)
    for l in range(w_ada.shape[0]):
        lam_init = 0.8 - 0.6 * math.exp(-0.3 * l)
        x = _layer(x, c, cos, sin, lam_init, w_ada[l], b_ada[l], norm1_g[l], w_in[l], b_gate[l],
                   qn_a[l], kn_a[l], w_proj_a[l], qn_b[l], kn_b[l], lam_q1[l], lam_k1[l], lam_q2[l],
                   lam_k2[l], subln_g[l], w_proj_b[l], w_out[l], norm2_g[l], w_query[l], sub_keys[l],
                   expert_u[l], expert_v[l])
    return x
```

```python
import functools
import math

import jax
import jax.numpy as jnp
from jax import lax
from jax.experimental import pallas as pl
from jax.experimental.pallas import tpu as pltpu

F32 = jnp.float32
BF16 = jnp.bfloat16

HEAD_DIM = 64
ROPE_THETA = 10000.0
EPS = 1e-6
NEG_INF = -1e30

DIL_GROUPS = ((128, 1), (512, 4), (2048, 16))
DIL_HEADS_PER_GROUP = 4
DIL_GROUP_WIDTH = DIL_HEADS_PER_GROUP * HEAD_DIM
DIL_SIDE = 64

DIFF_HEADS = 8
DIFF_TK = 512
LOG2E = 1.4426950408889634
DIFF_SCORE_BOUND = 60.0
PEER_HEADS = 8
PEER_N_KEYS = 128
PEER_TOPK = 16
PEER_HALF = 128

LANES = 128
VMEM_LIMIT = 56 * 1024 * 1024

_NT = (((1,), (1,)), ((), ()))


def _cparams(n_axes):
    return pltpu.CompilerParams(dimension_semantics=("arbitrary",) * n_axes,
                                vmem_limit_bytes=VMEM_LIMIT)


def _ada_kernel(c_ref, w_ref, b_ref, o_ref):
    c = c_ref[...]
    sc = (c * jax.nn.sigmoid(c)).astype(BF16)
    o_ref[...] = jnp.dot(sc, w_ref[...].astype(BF16), preferred_element_type=F32) + b_ref[...]


def _ada(c, w_ada, b_ada):
    bsz, d = c.shape
    n = w_ada.shape[1]
    tn = 1024
    return pl.pallas_call(
        _ada_kernel,
        out_shape=jax.ShapeDtypeStruct((bsz, n), F32),
        grid=(n // tn,),
        in_specs=[pl.BlockSpec((bsz, d), lambda j: (0, 0)),
                  pl.BlockSpec((d, tn), lambda j: (0, j)),
                  pl.BlockSpec((1, tn), lambda j: (0, j))],
        out_specs=pl.BlockSpec((bsz, tn), lambda j: (0, j)),
        compiler_params=_cparams(1),
        name="ada_mod",
    )(c, w_ada, b_ada.reshape(1, n))


def _rope_kernel(pos_ref, freq_ref, cos_ref, sin_ref):
    ang = pos_ref[...] * freq_ref[...]
    lane = lax.broadcasted_iota(jnp.int32, ang.shape, 1)
    first_half = (lane % HEAD_DIM) < (HEAD_DIM // 2)
    cos_ref[...] = jnp.cos(ang)
    s = jnp.sin(ang)
    sin_ref[...] = jnp.where(first_half, -s, s)


def _rope_tables(positions):
    t = positions.size
    tm = min(1024, t)
    pos = positions.reshape(t, 1).astype(F32)
    inv_freq = 1.0 / (ROPE_THETA ** (jnp.arange(0, HEAD_DIM, 2, dtype=F32) / HEAD_DIM))
    freq = jnp.tile(inv_freq, LANES // (HEAD_DIM // 2)).reshape(1, LANES)
    return pl.pallas_call(
        _rope_kernel,
        out_shape=(jax.ShapeDtypeStruct((t, LANES), F32), jax.ShapeDtypeStruct((t, LANES), F32)),
        grid=(t // tm,),
        in_specs=[pl.BlockSpec((tm, 1), lambda i: (i, 0)),
                  pl.BlockSpec((1, LANES), lambda i: (0, 0))],
        out_specs=(pl.BlockSpec((tm, LANES), lambda i: (i, 0)),
                   pl.BlockSpec((tm, LANES), lambda i: (i, 0))),
        compiler_params=_cparams(1),
        name="rope_tables",
    )(pos, freq)


def _modulated_norm(x, g, scale, shift):
    ms = jnp.mean(x * x, axis=-1, keepdims=True)
    return (x * lax.rsqrt(ms + EPS)) * g * (1.0 + scale) + shift


def _head_norm_rope(acc, gain, cos, sin, gmat):
    ss = jnp.dot((acc * acc).astype(BF16), gmat, preferred_element_type=F32)
    yn = acc * lax.rsqrt(ss * (1.0 / HEAD_DIM) + EPS) * gain
    lane = lax.broadcasted_iota(jnp.int32, cos.shape, 1)
    first_half = (lane % HEAD_DIM) < (HEAD_DIM // 2)
    outs = []
    for hf in range(acc.shape[1] // LANES):
        y = yn[:, hf * LANES:(hf + 1) * LANES]
        up = pltpu.roll(y, HEAD_DIM // 2, axis=1)
        down = pltpu.roll(y, LANES - HEAD_DIM // 2, axis=1)
        swapped = jnp.where(first_half, down, up)
        outs.append(y * cos + swapped * sin)
    return jnp.concatenate(outs, axis=1)


def _group_sum_matrix():
    r = lax.broadcasted_iota(jnp.int32, (2 * LANES, 2 * LANES), 0) // HEAD_DIM
    c = lax.broadcasted_iota(jnp.int32, (2 * LANES, 2 * LANES), 1) // HEAD_DIM
    return (r == c).astype(BF16)


def _inproj_a_kernel(x_ref, mod_ref, ng_ref, w_ref, gain_ref, cos_ref, sin_ref, gmat_ref,
                     o0_ref, o1_ref, o2_ref, h_sc, y_sc):
    j = pl.program_id(1)
    tm = x_ref.shape[0]

    @pl.when(j == 0)
    def _():
        m = mod_ref[0]
        h_sc[...] = _modulated_norm(x_ref[...], ng_ref[...], m[1:2], m[0:1]).astype(BF16)

    acc = jnp.dot(h_sc[...], w_ref[...], preferred_element_type=F32)
    outs = (o0_ref, o1_ref, o2_ref)

    def emit(get_piece):
        for g, (_, dil) in enumerate(DIL_GROUPS):
            y = get_piece(g)
            if dil == 1:
                outs[g][0, 0, 0] = y.astype(BF16)
            else:
                rows = tm // dil
                for hf in range(DIL_GROUP_WIDTH // LANES):
                    y_sc[hf] = y[:, hf * LANES:(hf + 1) * LANES]
                for r in range(dil):
                    piece = jnp.concatenate(
                        [y_sc[hf, pl.ds(r, rows, stride=dil), :] for hf in range(DIL_GROUP_WIDTH // LANES)],
                        axis=1)
                    outs[g][0, 0, r] = piece.astype(BF16)

    @pl.when(j < 2)
    def _():
        cos = cos_ref[...]
        sin = sin_ref[...]
        gmat = gmat_ref[...]
        emit(lambda g: _head_norm_rope(acc[:, g * DIL_GROUP_WIDTH:(g + 1) * DIL_GROUP_WIDTH],
                                       gain_ref[:, g * DIL_GROUP_WIDTH:(g + 1) * DIL_GROUP_WIDTH],
                                       cos, sin, gmat))

    @pl.when(j == 2)
    def _():
        emit(lambda g: acc[:, g * DIL_GROUP_WIDTH:(g + 1) * DIL_GROUP_WIDTH])


def _inproj_a(x2, mod3, norm_g, w_a, gain_a, cos, sin, gmat, bsz, seq):
    t, d = x2.shape
    tm = min(1024, seq)
    spt = seq // tm
    n = w_a.shape[1]
    tn = n // 3
    out_shapes, out_specs = [], []
    for _, dil in DIL_GROUPS:
        out_shapes.append(jax.ShapeDtypeStruct((3, bsz, dil, seq // dil, DIL_GROUP_WIDTH), BF16))
        out_specs.append(pl.BlockSpec((1, 1, dil, tm // dil, DIL_GROUP_WIDTH),
                                      lambda i, j: (j, i // spt, 0, i % spt, 0)))
    return pl.pallas_call(
        _inproj_a_kernel,
        out_shape=tuple(out_shapes),
        grid=(t // tm, 3),
        in_specs=[pl.BlockSpec((tm, d), lambda i, j: (i, 0)),
                  pl.BlockSpec((1, 6, d), lambda i, j: (i // spt, 0, 0)),
                  pl.BlockSpec((1, d), lambda i, j: (0, 0)),
                  pl.BlockSpec((d, tn), lambda i, j: (0, j)),
                  pl.BlockSpec((1, tn), lambda i, j: (0, j)),
                  pl.BlockSpec((tm, LANES), lambda i, j: (i, 0)),
                  pl.BlockSpec((tm, LANES), lambda i, j: (i, 0)),
                  pl.BlockSpec((2 * LANES, 2 * LANES), lambda i, j: (0, 0))],
        out_specs=tuple(out_specs),
        scratch_shapes=[pltpu.VMEM((tm, d), BF16),
                        pltpu.VMEM((DIL_GROUP_WIDTH // LANES, tm, LANES), F32)],
        compiler_params=_cparams(2),
        name="inproj_dilated",
    )(x2, mod3, norm_g, w_a, gain_a, cos, sin, gmat)


def _inproj_b_kernel(x_ref, mod_ref, ng_ref, w_ref, gain_ref, bias_ref, cos_ref, sin_ref, gmat_ref,
                     qk_ref, vt_ref, gate_ref, h_sc, *, n_rope_tiles, n_plain_tiles):
    j = pl.program_id(1)

    @pl.when(j == 0)
    def _():
        m = mod_ref[0]
        h_sc[...] = _modulated_norm(x_ref[...], ng_ref[...], m[1:2], m[0:1]).astype(BF16)

    acc = jnp.dot(h_sc[...], w_ref[...], preferred_element_type=F32)
    tn = acc.shape[1]

    @pl.when(j < n_rope_tiles)
    def _():
        cos = cos_ref[...]
        sin = sin_ref[...]
        gmat = gmat_ref[...]
        for p in range(tn // (2 * LANES)):
            sl = slice(p * 2 * LANES, (p + 1) * 2 * LANES)
            qk_ref[:, sl] = _head_norm_rope(acc[:, sl], gain_ref[:, sl], cos, sin, gmat).astype(BF16)

    @pl.when((j >= n_rope_tiles) & (j < n_rope_tiles + n_plain_tiles))
    def _():
        for cc in range(vt_ref.shape[0]):
            vt_ref[cc] = acc[cc * DIFF_TK:(cc + 1) * DIFF_TK, :].T.astype(BF16)

    @pl.when(j >= n_rope_tiles + n_plain_tiles)
    def _():
        gate_ref[...] = jax.nn.sigmoid(acc + bias_ref[...]).astype(BF16)


def _inproj_b(x2, mod3, norm_g, w_b, gain_b, bias_b, cos, sin, gmat, seq, qk_width, v_width):
    t, d = x2.shape
    tm = min(1024, seq)
    spt = seq // tm
    n = w_b.shape[1]
    tn = 512
    n_rope, n_plain = 2 * qk_width // tn, v_width // tn
    n_gate = n // tn - n_rope - n_plain
    kern = functools.partial(_inproj_b_kernel, n_rope_tiles=n_rope, n_plain_tiles=n_plain)
    return pl.pallas_call(
        kern,
        out_shape=(jax.ShapeDtypeStruct((t, n_rope * tn), BF16),
                   jax.ShapeDtypeStruct((t // DIFF_TK, v_width, DIFF_TK), BF16),
                   jax.ShapeDtypeStruct((t, n_gate * tn), BF16)),
        grid=(t // tm, n // tn),
        in_specs=[pl.BlockSpec((tm, d), lambda i, j: (i, 0)),
                  pl.BlockSpec((1, 6, d), lambda i, j: (i // spt, 0, 0)),
                  pl.BlockSpec((1, d), lambda i, j: (0, 0)),
                  pl.BlockSpec((d, tn), lambda i, j: (0, j)),
                  pl.BlockSpec((1, tn), lambda i, j: (0, j)),
                  pl.BlockSpec((1, tn), lambda i, j: (0, j)),
                  pl.BlockSpec((tm, LANES), lambda i, j: (i, 0)),
                  pl.BlockSpec((tm, LANES), lambda i, j: (i, 0)),
                  pl.BlockSpec((2 * LANES, 2 * LANES), lambda i, j: (0, 0))],
        out_specs=(pl.BlockSpec((tm, tn), lambda i, j: (i, jnp.minimum(j, n_rope - 1))),
                   pl.BlockSpec((tm // DIFF_TK, tn, DIFF_TK),
                                lambda i, j: (i, jnp.clip(j - n_rope, 0, n_plain - 1), 0)),
                   pl.BlockSpec((tm, tn), lambda i, j: (i, jnp.clip(j - n_rope - n_plain, 0, n_gate - 1)))),
        scratch_shapes=[pltpu.VMEM((tm, d), BF16)],
        compiler_params=_cparams(2),
        name="inproj_diff_gates",
    )(x2, mod3, norm_g, w_b, gain_b, bias_b, cos, sin, gmat)


def _dilated_attn_kernel(q_ref, k_ref, v_ref, o_ref, lse_ref, *, tq, kw, length):
    qi = pl.program_id(1)
    q0 = qi * tq
    kstart = jnp.clip(q0 - DIL_SIDE, 0, length - kw)
    kstart = pl.multiple_of(kstart, DIL_SIDE)
    q = q_ref[0, 0]
    k = k_ref[0, 0, pl.ds(kstart, kw), :]
    v = v_ref[0, 0, pl.ds(kstart, kw), :]
    lane_head = lax.broadcasted_iota(jnp.int32, (tq, DIL_GROUP_WIDTH), 1) // HEAD_DIM
    qidx = q0 + lax.broadcasted_iota(jnp.int32, (tq, kw), 0)
    kidx = kstart + lax.broadcasted_iota(jnp.int32, (tq, kw), 1)
    valid = jnp.abs(kidx - qidx) <= DIL_SIDE
    out = jnp.zeros((tq, DIL_GROUP_WIDTH), F32)
    lse_full = jnp.zeros((tq, DIL_GROUP_WIDTH), F32)
    for h in range(DIL_HEADS_PER_GROUP):
        in_head = lane_head == h
        qh = jnp.where(in_head, q, jnp.zeros_like(q))
        s = lax.dot_general(qh, k, _NT, preferred_element_type=F32)
        s = jnp.where(valid, s, NEG_INF)
        m = jnp.max(s, axis=-1, keepdims=True)
        p = jnp.exp(s - m)
        l = jnp.sum(p, axis=-1, keepdims=True)
        oh = jnp.dot(p.astype(BF16), v, preferred_element_type=F32)
        out = jnp.where(in_head, oh / l, out)
        lse_full = jnp.where(in_head, m + jnp.log(l), lse_full)
    o_ref[0] = out.astype(BF16)
    lse_ref[0] = lse_full


def _dilated_attn(qkv):
    _, bsz, dil, length, w = qkv.shape
    bd = bsz * dil
    qkv = qkv.reshape(3, bd, length, w)
    tq = min(128, length)
    kw = min(tq + 2 * DIL_SIDE, length)
    kern = functools.partial(_dilated_attn_kernel, tq=tq, kw=kw, length=length)
    return pl.pallas_call(
        kern,
        out_shape=(jax.ShapeDtypeStruct((bd, length, w), BF16), jax.ShapeDtypeStruct((bd, length, w), F32)),
        grid=(bd, length // tq),
        in_specs=[pl.BlockSpec((1, 1, tq, w), lambda b, i: (0, b, i, 0)),
                  pl.BlockSpec((1, 1, length, w), lambda b, i: (1, b, 0, 0)),
                  pl.BlockSpec((1, 1, length, w), lambda b, i: (2, b, 0, 0))],
        out_specs=(pl.BlockSpec((1, tq, w), lambda b, i: (b, i, 0)),
                   pl.BlockSpec((1, tq, w), lambda b, i: (b, i, 0))),
        compiler_params=_cparams(2),
        name="dilated_attn",
    )(qkv, qkv, qkv)


def _dil_combine_kernel(o0_ref, l0_ref, o1_ref, l1_ref, o2_ref, l2_ref, oa_ref, nat_sc):
    tm = oa_ref.shape[0]
    n_half = DIL_GROUP_WIDTH // LANES

    def natural(ref, dil):
        if dil == 1:
            return ref[0, 0].astype(F32)
        rows = tm // dil
        for r in range(dil):
            blk = ref[0, r].astype(F32)
            for hf in range(n_half):
                nat_sc[hf, pl.ds(r, rows, stride=dil), :] = blk[:, hf * LANES:(hf + 1) * LANES]
        return jnp.concatenate([nat_sc[hf] for hf in range(n_half)], axis=1)

    dils = [d for _, d in DIL_GROUPS]
    lses = [natural(r, d) for r, d in zip((l0_ref, l1_ref, l2_ref), dils)]
    m = jnp.maximum(jnp.maximum(lses[0], lses[1]), lses[2])
    ws = [jnp.exp(l - m) for l in lses]
    den = ws[0] + ws[1] + ws[2]
    num = jnp.zeros_like(den)
    for w, r, d in zip(ws, (o0_ref, o1_ref, o2_ref), dils):
        num = num + w * natural(r, d)
    oa_ref[...] = (num / den).astype(BF16)


def _dil_combine(outs, lses, bsz, seq):
    tm = min(512, seq)
    spt = seq // tm
    in_specs, args = [], []
    for (o, l), (_, dil) in zip(zip(outs, lses), DIL_GROUPS):
        shape4 = (bsz, dil, seq // dil, DIL_GROUP_WIDTH)
        spec = pl.BlockSpec((1, dil, tm // dil, DIL_GROUP_WIDTH), lambda i: (i // spt, 0, i % spt, 0))
        in_specs += [spec, spec]
        args += [o.reshape(shape4), l.reshape(shape4)]
    return pl.pallas_call(
        _dil_combine_kernel,
        out_shape=jax.ShapeDtypeStruct((bsz * seq, DIL_GROUP_WIDTH), BF16),
        grid=(bsz * spt,),
        in_specs=in_specs,
        out_specs=pl.BlockSpec((tm, DIL_GROUP_WIDTH), lambda i: (i, 0)),
        scratch_shapes=[pltpu.VMEM((DIL_GROUP_WIDTH // LANES, tm, LANES), F32)],
        compiler_params=_cparams(1),
        name="dilated_combine",
    )(*args)


def _diff_attn_kernel(bounded_ref, q_ref, k_ref, vt_ref, lq1_ref, lk1_ref, lq2_ref, lk2_ref, sg_ref, o_ref,
                      l_sc, acc_sc, *, lam_init):
    tq = q_ref.shape[0]
    n_chunks = vt_ref.shape[0]
    q = q_ref[...]
    lane = lax.broadcasted_iota(jnp.int32, q.shape, 1)
    zero = jnp.zeros_like(q)
    qz = jnp.concatenate([jnp.where(lane < HEAD_DIM, q, zero), jnp.where(lane >= HEAD_DIM, q, zero)], axis=0)

    def scores(c):
        ks = pl.multiple_of(c * DIFF_TK, DIFF_TK)
        k = k_ref[pl.ds(ks, DIFF_TK), :]
        return lax.dot_general(k, qz, _NT, preferred_element_type=F32)

    @pl.when(bounded_ref[0] != 0)
    def _():
        l = jnp.zeros((1, 2 * tq), F32)
        acc = jnp.zeros((vt_ref.shape[1], 2 * tq), F32)
        for c in range(n_chunks):
            p = jnp.exp2(scores(c))
            l = l + jnp.sum(p, axis=0, keepdims=True)
            acc = acc + jnp.dot(vt_ref[c], p.astype(BF16), preferred_element_type=F32)
        l_sc[...] = l
        acc_sc[...] = acc

    @pl.when(bounded_ref[0] == 0)
    def _():
        def body(c, carry):
            m, l, acc = carry
            st = scores(c)
            m_new = jnp.maximum(m, jnp.max(st, axis=0, keepdims=True))
            alpha = jnp.exp2(m - m_new)
            p = jnp.exp2(st - m_new)
            l = alpha * l + jnp.sum(p, axis=0, keepdims=True)
            acc = alpha * acc + jnp.dot(vt_ref[c], p.astype(BF16), preferred_element_type=F32)
            return m_new, l, acc

        init = (jnp.full((1, 2 * tq), -jnp.inf, F32), jnp.zeros((1, 2 * tq), F32),
                jnp.zeros((vt_ref.shape[1], 2 * tq), F32))
        _, l, acc = lax.fori_loop(0, n_chunks, body, init)
        l_sc[...] = l
        acc_sc[...] = acc

    o = acc_sc[...] / l_sc[...]
    lam = (jnp.exp(jnp.sum(lq1_ref[...] * lk1_ref[...], axis=-1, keepdims=True))
           - jnp.exp(jnp.sum(lq2_ref[...] * lk2_ref[...], axis=-1, keepdims=True)) + lam_init)
    obt = o[:, :tq] - lam * o[:, tq:]
    ms = jnp.mean(obt * obt, axis=0, keepdims=True)
    obn = obt * lax.rsqrt(ms + EPS)
    o_ref[...] = (obn.T * (sg_ref[...] * (1.0 - lam_init))).astype(BF16)


def _diff_attn(qk, vt, score_bounded, lam_vecs, subln_g, bsz, seq, lam_init):
    t = qk.shape[0]
    tq = min(256, seq)
    qpt = seq // tq
    cps = seq // DIFF_TK
    hw = 2 * HEAD_DIM
    kern = functools.partial(_diff_attn_kernel, lam_init=lam_init)
    vec_spec = pl.BlockSpec((1, HEAD_DIM), lambda b, h, i: (0, 0))
    return pl.pallas_call(
        kern,
        out_shape=jax.ShapeDtypeStruct((t, DIFF_HEADS * hw), BF16),
        grid=(bsz, DIFF_HEADS, qpt),
        in_specs=[pl.BlockSpec(memory_space=pltpu.SMEM),
                  pl.BlockSpec((tq, hw), lambda b, h, i: (b * qpt + i, h)),
                  pl.BlockSpec((seq, hw), lambda b, h, i: (b, DIFF_HEADS + h)),
                  pl.BlockSpec((cps, hw, DIFF_TK), lambda b, h, i: (b, h, 0)),
                  vec_spec, vec_spec, vec_spec, vec_spec,
                  pl.BlockSpec((1, hw), lambda b, h, i: (0, 0))],
        out_specs=pl.BlockSpec((tq, hw), lambda b, h, i: (b * qpt + i, h)),
        scratch_shapes=[pltpu.VMEM((1, 2 * tq), F32), pltpu.VMEM((hw, 2 * tq), F32)],
        compiler_params=_cparams(3),
        name="diff_attn",
    )(score_bounded, qk, qk, vt, *lam_vecs, subln_g)


def _merge_kernel(oa_ref, ob_ref, ga_ref, gb_ref, x_ref, mod_ref, ng_ref, wpa_ref, wpb_ref, wo_ref,
                  x1_ref, h2_ref):
    ba = jnp.dot(oa_ref[...], wpa_ref[...], preferred_element_type=F32)
    bb = jnp.dot(ob_ref[...], wpb_ref[...], preferred_element_type=F32)
    mixed = ga_ref[...].astype(F32) * ba + gb_ref[...].astype(F32) * bb
    mo = jnp.dot(mixed.astype(BF16), wo_ref[...], preferred_element_type=F32)
    m = mod_ref[0]
    x1 = x_ref[...] + m[2:3] * mo
    x1_ref[...] = x1
    h2_ref[...] = _modulated_norm(x1, ng_ref[...], m[4:5], m[3:4]).astype(BF16)


def _merge(oa, ob, gates, x2, mod3, norm2_g, wpa, wpb, wo, seq):
    t, d = x2.shape
    tm = min(512, seq)
    spt = seq // tm
    full = lambda i: (0, 0)
    return pl.pallas_call(
        _merge_kernel,
        out_shape=(jax.ShapeDtypeStruct((t, d), F32), jax.ShapeDtypeStruct((t, d), BF16)),
        grid=(t // tm,),
        in_specs=[pl.BlockSpec((tm, oa.shape[1]), lambda i: (i, 0)),
                  pl.BlockSpec((tm, d), lambda i: (i, 0)),
                  pl.BlockSpec((tm, d), lambda i: (i, 0)),
                  pl.BlockSpec((tm, d), lambda i: (i, 1)),
                  pl.BlockSpec((tm, d), lambda i: (i, 0)),
                  pl.BlockSpec((1, 6, d), lambda i: (i // spt, 0, 0)),
                  pl.BlockSpec((1, d), full),
                  pl.BlockSpec(wpa.shape, full),
                  pl.BlockSpec(wpb.shape, full),
                  pl.BlockSpec(wo.shape, full)],
        out_specs=(pl.BlockSpec((tm, d), lambda i: (i, 0)), pl.BlockSpec((tm, d), lambda i: (i, 0))),
        compiler_params=_cparams(1),
        name="merge_outproj_norm2",
    )(oa, ob, gates, gates, x2, mod3, norm2_g, wpa, wpb, wo)


def _extract_topk(s, idx_f, val_sc, idx_sc):
    for k in range(PEER_TOPK):
        m = jnp.max(s, axis=0, keepdims=True)
        idx = jnp.min(jnp.where(s == m, idx_f, 1e9), axis=0, keepdims=True)
        val_sc[k:k + 1, :] = m
        idx_sc[k:k + 1, :] = idx
        s = jnp.where(idx_f == idx, -jnp.inf, s)


def _peer_topk_kernel(h_ref, wq_ref, sk_ref, e1_ref, e2_ref, g_ref,
                      q_sc, v1_sc, i1_sc, v2_sc, i2_sc, top_sc, lab_sc, e1t_sc, e2t_sc, gt_sc):
    tm = h_ref.shape[0]
    q = jnp.dot(h_ref[...], wq_ref[...], preferred_element_type=F32).astype(BF16)
    for piece in range(2 * PEER_HEADS):
        q_sc[piece] = q[:, piece * PEER_HALF:(piece + 1) * PEER_HALF]
    key_idx = lax.broadcasted_iota(jnp.int32, (PEER_N_KEYS, tm), 0).astype(F32)
    sub8 = lax.broadcasted_iota(jnp.int32, (8, tm), 0).astype(F32)
    neg = jnp.full((8, tm), -jnp.inf, F32)

    def head(h, carry):
        for side, (v_sc, i_sc) in enumerate(((v1_sc, i1_sc), (v2_sc, i2_sc))):
            qs = q_sc[2 * h + side]
            st = lax.dot_general(sk_ref[2 * h + side], qs, _NT, preferred_element_type=F32)
            _extract_topk(st, key_idx, v_sc, i_sc)
        v2lo = v2_sc[0:8, :]
        cands = [v1_sc[0:1, :] + v2lo, v1_sc[0:1, :] + v2_sc[8:16, :]]
        labels = [sub8, sub8 + 8.0]
        for a in range(1, 8):
            nb = PEER_TOPK // (a + 1)
            cands.append(jnp.where(sub8 < nb, v1_sc[a:a + 1, :] + v2lo, neg))
            labels.append(sub8 + float(a * PEER_TOPK))
        cands.append(v1_sc[8:16, :] + v2_sc[0:1, :])
        labels.append((sub8 + 8.0) * float(PEER_TOPK))
        cand = jnp.concatenate(cands, axis=0)
        label = jnp.concatenate(labels, axis=0)
        _extract_topk(cand, label, top_sc, lab_sc)
        top = top_sc[...]
        lab = lab_sc[...]
        a_sel = jnp.floor(lab * (1.0 / PEER_TOPK))
        b_sel = lab - a_sel * PEER_TOPK
        e1 = jnp.zeros_like(top)
        e2 = jnp.zeros_like(top)
        for r in range(PEER_TOPK):
            e1 = jnp.where(a_sel == float(r), i1_sc[r:r + 1, :], e1)
            e2 = jnp.where(b_sel == float(r), i2_sc[r:r + 1, :], e2)
        p = jnp.exp(top - jnp.max(top, axis=0, keepdims=True))
        gate = p / jnp.sum(p, axis=0, keepdims=True)
        row = pl.multiple_of(h * PEER_TOPK, PEER_TOPK)
        e1t_sc[pl.ds(row, PEER_TOPK), :] = e1
        e2t_sc[pl.ds(row, PEER_TOPK), :] = e2
        gt_sc[pl.ds(row, PEER_TOPK), :] = gate
        return carry

    lax.fori_loop(0, PEER_HEADS, head, 0)
    e1_ref[...] = e1t_sc[...].T
    e2_ref[...] = e2t_sc[...].T
    g_ref[...] = gt_sc[...].T


def _peer_topk(h2, wq, sk):
    t, d = h2.shape
    tm = min(256, t)
    nsel = PEER_HEADS * PEER_TOPK
    out = jax.ShapeDtypeStruct((t, nsel), F32)
    spec = pl.BlockSpec((tm, nsel), lambda i: (i, 0))
    small = lambda: pltpu.VMEM((PEER_TOPK, tm), F32)
    big = lambda: pltpu.VMEM((nsel, tm), F32)
    return pl.pallas_call(
        _peer_topk_kernel,
        out_shape=(out, out, out),
        grid=(t // tm,),
        in_specs=[pl.BlockSpec((tm, d), lambda i: (i, 0)),
                  pl.BlockSpec(wq.shape, lambda i: (0, 0)),
                  pl.BlockSpec(sk.shape, lambda i: (0, 0, 0))],
        out_specs=(spec, spec, spec),
        scratch_shapes=[pltpu.VMEM((2 * PEER_HEADS, tm, PEER_HALF), BF16),
                        small(), small(), small(), small(), small(), small(), big(), big(), big()],
        compiler_params=_cparams(1),
        name="peer_topk",
    )(h2, wq, sk)


def _peer_gates_kernel(e1_ref, e2_ref, g_ref, o_ref, g_sc, *, pitch):
    tm = e1_ref.shape[0]
    row_idx = lax.broadcasted_iota(jnp.int32, (PEER_N_KEYS, e1_ref.shape[1]), 0).astype(F32)

    def token(t, carry):
        r1 = e1_ref[pl.ds(t, 1), :]
        r2 = e2_ref[pl.ds(t, 1), :]
        gt = g_ref[pl.ds(t, 1), :]
        sel_i = jnp.where(row_idx == r1, 1.0, 0.0).astype(BF16)
        sel_j = jnp.where(row_idx == r2, gt, 0.0).astype(BF16)
        gm = lax.dot_general(sel_i, sel_j, _NT, preferred_element_type=F32)
        g_sc[pl.ds(t, PEER_N_KEYS, stride=pitch), :] = gm
        return carry

    lax.fori_loop(0, tm, token, 0, unroll=8)

    def emit(i, carry):
        start = pl.multiple_of(i * pitch, 8)
        o_ref[i] = g_sc[pl.ds(start, tm), :].astype(BF16)
        return carry

    lax.fori_loop(0, PEER_N_KEYS, emit, 0)


def _peer_gates(e1, e2, gate):
    t, nsel = e1.shape
    tm = min(128, t)
    pitch = tm + 8
    spec = pl.BlockSpec((tm, nsel), lambda i: (i, 0))
    return pl.pallas_call(
        functools.partial(_peer_gates_kernel, pitch=pitch),
        out_shape=jax.ShapeDtypeStruct((PEER_N_KEYS, t, PEER_N_KEYS), BF16),
        grid=(t // tm,),
        in_specs=[spec, spec, spec],
        out_specs=pl.BlockSpec((PEER_N_KEYS, tm, PEER_N_KEYS), lambda i: (0, i, 0)),
        scratch_shapes=[pltpu.VMEM((PEER_N_KEYS * pitch, PEER_N_KEYS), F32)],
        compiler_params=_cparams(1),
        name="peer_gate_matrix",
    )(e1, e2, gate)


def _peer_dense_kernel(h_ref, u_ref, v_ref, g_ref, x1_ref, mod_ref, o_ref, acc_sc):
    c = pl.program_id(1)

    @pl.when(c == 0)
    def _():
        acc_sc[...] = jnp.zeros_like(acc_sc)

    a = lax.dot_general(h_ref[...], u_ref[...], _NT, preferred_element_type=F32)
    act = 0.5 * a * (1.0 + lax.erf(a * (1.0 / math.sqrt(2.0))))
    ws = [(act[:, ib * LANES:(ib + 1) * LANES] * g_ref[ib].astype(F32)).astype(BF16)
          for ib in range(g_ref.shape[0])]
    w = jnp.concatenate(ws, axis=1)
    acc_sc[...] += jnp.dot(w, v_ref[...], preferred_element_type=F32)

    @pl.when(c == pl.num_programs(1) - 1)
    def _():
        o_ref[...] = x1_ref[...] + mod_ref[0][5:6] * acc_sc[...]


def _peer_dense(h2, u, v, gmat, x1, mod3, seq):
    t, d = h2.shape
    n_exp = u.shape[0]
    tm = min(1024, seq)
    spt = seq // tm
    tn = 512
    return pl.pallas_call(
        _peer_dense_kernel,
        out_shape=jax.ShapeDtypeStruct((t, d), F32),
        grid=(t // tm, n_exp // tn),
        in_specs=[pl.BlockSpec((tm, d), lambda i, c: (i, 0)),
                  pl.BlockSpec((tn, d), lambda i, c: (c, 0)),
                  pl.BlockSpec((tn, d), lambda i, c: (c, 0)),
                  pl.BlockSpec((tn // PEER_N_KEYS, tm, PEER_N_KEYS), lambda i, c: (c, i, 0)),
                  pl.BlockSpec((tm, d), lambda i, c: (i, 0)),
                  pl.BlockSpec((1, 6, d), lambda i, c: (i // spt, 0, 0))],
        out_specs=pl.BlockSpec((tm, d), lambda i, c: (i, 0)),
        scratch_shapes=[pltpu.VMEM((tm, d), F32)],
        compiler_params=_cparams(2),
        name="peer_dense",
    )(h2, u, v, gmat, x1, mod3)


def _layer(x, c, cos, sin, lam_init, w_ada, b_ada, norm1_g, w_in, b_gate, qn_a, kn_a, w_proj_a,
           qn_b, kn_b, lam_q1, lam_k1, lam_q2, lam_k2, subln_g, w_proj_b, w_out, norm2_g,
           w_query, sub_keys, expert_u, expert_v):
    bsz, seq, d = x.shape
    t = bsz * seq
    x2 = x.reshape(t, d)
    mod3 = _ada(c, w_ada, b_ada).reshape(bsz, 6, d)
    gmat = _group_sum_matrix()
    scale = HEAD_DIM ** -0.5

    dil_w = len(DIL_GROUPS) * DIL_GROUP_WIDTH
    diff_w = DIFF_HEADS * 2 * HEAD_DIM
    a_cols = 3 * dil_w
    w_in_bf = w_in.astype(BF16)
    ones = lambda n: jnp.ones((n,), F32)
    gain_a = jnp.concatenate([jnp.tile(qn_a, dil_w // HEAD_DIM) * scale, jnp.tile(kn_a, dil_w // HEAD_DIM),
                              ones(dil_w)]).reshape(1, a_cols)
    qkv_groups = _inproj_a(x2, mod3, norm1_g.reshape(1, d), w_in_bf[:, :a_cols], gain_a, cos, sin, gmat, bsz, seq)

    n_b = w_in.shape[1] - a_cols
    gain_b = jnp.concatenate([jnp.tile(qn_b, diff_w // HEAD_DIM) * (scale * LOG2E),
                              jnp.tile(kn_b, diff_w // HEAD_DIM), ones(n_b - 2 * diff_w)]).reshape(1, n_b)
    bias_b = jnp.concatenate([jnp.zeros((3 * diff_w,), F32), b_gate]).reshape(1, n_b)
    qk_b, vt_b, gates = _inproj_b(x2, mod3, norm1_g.reshape(1, d), w_in_bf[:, a_cols:], gain_b, bias_b,
                                  cos, sin, gmat, seq, diff_w, diff_w)

    outs, lses = [], []
    for qkv in qkv_groups:
        o, l = _dilated_attn(qkv)
        outs.append(o)
        lses.append(l)
    oa = _dil_combine(outs, lses, bsz, seq)

    lam_vecs = [v.reshape(1, HEAD_DIM) for v in (lam_q1, lam_k1, lam_q2, lam_k2)]
    score_cap = (HEAD_DIM * scale * LOG2E * 1.01) * jnp.max(jnp.abs(qn_b)) * jnp.max(jnp.abs(kn_b))
    score_bounded = (score_cap <= DIFF_SCORE_BOUND).astype(jnp.int32).reshape(1)
    ob = _diff_attn(qk_b, vt_b, score_bounded, lam_vecs, subln_g.reshape(1, 2 * HEAD_DIM), bsz, seq, lam_init)

    x1, h2 = _merge(oa, ob, gates, x2, mod3, norm2_g.reshape(1, d), w_proj_a.astype(BF16),
                    w_proj_b.astype(BF16), w_out.astype(BF16), seq)

    sk = sub_keys.astype(BF16).reshape(PEER_HEADS * 2, PEER_N_KEYS, PEER_HALF)
    e1, e2, gate = _peer_topk(h2, w_query.astype(BF16), sk)
    gdense = _peer_gates(e1, e2, gate)
    out = _peer_dense(h2, expert_u.astype(BF16), expert_v.astype(BF16), gdense, x1, mod3, seq)
    return out.reshape(bsz, seq, d)


def kernel(x, c, positions, w_ada, b_ada, norm1_g, w_in, b_gate, qn_a, kn_a, w_proj_a, qn_b, kn_b,
           lam_q1, lam_k1, lam_q2, lam_k2, subln_g, w_proj_b, w_out, norm2_g, w_query, sub_keys,
           expert_u, expert_v):
    cos, sin = _rope_tables(positions)
    for l in range(w_ada.shape[0]):
        lam_init = 0.8 - 0.6 * math.exp(-0.3 * l)
        x = _layer(x, c, cos, sin, lam_init, w_ada[l], b_ada[l], norm1_g[l], w_in[l], b_gate[l],
                   qn_a[l], kn_a[l], w_proj_a[l], qn_b[l], kn_b[l], lam_q1[l], lam_k1[l], lam_q2[l],
                   lam_k2[l], subln_g[l], w_proj_b[l], w_out[l], norm2_g[l], w_query[l], sub_keys[l],
                   expert_u[l], expert_v[l])
    return x
```

```python
import functools
import math

import jax
import jax.numpy as jnp
from jax import lax
from jax.experimental import pallas as pl
from jax.experimental.pallas import tpu as pltpu

F32 = jnp.float32
BF16 = jnp.bfloat16

HEAD_DIM = 64
ROPE_THETA = 10000.0
EPS = 1e-6
NEG_INF = -1e30

DIL_GROUPS = ((128, 1), (512, 4), (2048, 16))
DIL_HEADS_PER_GROUP = 4
DIL_GROUP_WIDTH = DIL_HEADS_PER_GROUP * HEAD_DIM
DIL_SIDE = 64

DIFF_HEADS = 8
DIFF_TK = 512
LOG2E = 1.4426950408889634
DIFF_SCORE_BOUND = 60.0
PEER_HEADS = 8
PEER_N_KEYS = 128
PEER_TOPK = 16
PEER_HALF = 128

LANES = 128
VMEM_LIMIT = 56 * 1024 * 1024

_NT = (((1,), (1,)), ((), ()))


def _cparams(n_axes):
    return pltpu.CompilerParams(dimension_semantics=("arbitrary",) * n_axes,
                                vmem_limit_bytes=VMEM_LIMIT)


def _ada_kernel(c_ref, w_ref, b_ref, o_ref):
    c = c_ref[...]
    sc = (c * jax.nn.sigmoid(c)).astype(BF16)
    o_ref[...] = jnp.dot(sc, w_ref[...].astype(BF16), preferred_element_type=F32) + b_ref[...]


def _ada(c, w_ada, b_ada):
    bsz, d = c.shape
    n = w_ada.shape[1]
    tn = 1024
    return pl.pallas_call(
        _ada_kernel,
        out_shape=jax.ShapeDtypeStruct((bsz, n), F32),
        grid=(n // tn,),
        in_specs=[pl.BlockSpec((bsz, d), lambda j: (0, 0)),
                  pl.BlockSpec((d, tn), lambda j: (0, j)),
                  pl.BlockSpec((1, tn), lambda j: (0, j))],
        out_specs=pl.BlockSpec((bsz, tn), lambda j: (0, j)),
        compiler_params=_cparams(1),
        name="ada_mod",
    )(c, w_ada, b_ada.reshape(1, n))


def _rope_kernel(pos_ref, freq_ref, cos_ref, sin_ref):
    ang = pos_ref[...] * freq_ref[...]
    lane = lax.broadcasted_iota(jnp.int32, ang.shape, 1)
    first_half = (lane % HEAD_DIM) < (HEAD_DIM // 2)
    cos_ref[...] = jnp.cos(ang)
    s = jnp.sin(ang)
    sin_ref[...] = jnp.where(first_half, -s, s)


def _rope_tables(positions):
    t = positions.size
    tm = min(1024, t)
    pos = positions.reshape(t, 1).astype(F32)
    inv_freq = 1.0 / (ROPE_THETA ** (jnp.arange(0, HEAD_DIM, 2, dtype=F32) / HEAD_DIM))
    freq = jnp.tile(inv_freq, LANES // (HEAD_DIM // 2)).reshape(1, LANES)
    return pl.pallas_call(
        _rope_kernel,
        out_shape=(jax.ShapeDtypeStruct((t, LANES), F32), jax.ShapeDtypeStruct((t, LANES), F32)),
        grid=(t // tm,),
        in_specs=[pl.BlockSpec((tm, 1), lambda i: (i, 0)),
                  pl.BlockSpec((1, LANES), lambda i: (0, 0))],
        out_specs=(pl.BlockSpec((tm, LANES), lambda i: (i, 0)),
                   pl.BlockSpec((tm, LANES), lambda i: (i, 0))),
        compiler_params=_cparams(1),
        name="rope_tables",
    )(pos, freq)


def _modulated_norm(x, g, scale, shift):
    ms = jnp.mean(x * x, axis=-1, keepdims=True)
    return (x * lax.rsqrt(ms + EPS)) * g * (1.0 + scale) + shift


def _head_norm_rope(acc, gain, cos, sin, gmat):
    ss = jnp.dot((acc * acc).astype(BF16), gmat, preferred_element_type=F32)
    yn = acc * lax.rsqrt(ss * (1.0 / HEAD_DIM) + EPS) * gain
    lane = lax.broadcasted_iota(jnp.int32, cos.shape, 1)
    first_half = (lane % HEAD_DIM) < (HEAD_DIM // 2)
    outs = []
    for hf in range(acc.shape[1] // LANES):
        y = yn[:, hf * LANES:(hf + 1) * LANES]
        up = pltpu.roll(y, HEAD_DIM // 2, axis=1)
        down = pltpu.roll(y, LANES - HEAD_DIM // 2, axis=1)
        swapped = jnp.where(first_half, down, up)
        outs.append(y * cos + swapped * sin)
    return jnp.concatenate(outs, axis=1)


def _group_sum_matrix():
    r = lax.broadcasted_iota(jnp.int32, (2 * LANES, 2 * LANES), 0) // HEAD_DIM
    c = lax.broadcasted_iota(jnp.int32, (2 * LANES, 2 * LANES), 1) // HEAD_DIM
    return (r == c).astype(BF16)


def _inproj_a_kernel(x_ref, mod_ref, ng_ref, w_ref, gain_ref, cos_ref, sin_ref, gmat_ref,
                     o0_ref, o1_ref, o2_ref, h_sc, y_sc):
    j = pl.program_id(1)
    tm = x_ref.shape[0]

    @pl.when(j == 0)
    def _():
        m = mod_ref[0]
        h_sc[...] = _modulated_norm(x_ref[...], ng_ref[...], m[1:2], m[0:1]).astype(BF16)

    acc = jnp.dot(h_sc[...], w_ref[...], preferred_element_type=F32)
    outs = (o0_ref, o1_ref, o2_ref)

    def emit(get_piece):
        for g, (_, dil) in enumerate(DIL_GROUPS):
            y = get_piece(g)
            if dil == 1:
                outs[g][0, 0, 0] = y.astype(BF16)
            else:
                rows = tm // dil
                for hf in range(DIL_GROUP_WIDTH // LANES):
                    y_sc[hf] = y[:, hf * LANES:(hf + 1) * LANES]
                for r in range(dil):
                    piece = jnp.concatenate(
                        [y_sc[hf, pl.ds(r, rows, stride=dil), :] for hf in range(DIL_GROUP_WIDTH // LANES)],
                        axis=1)
                    outs[g][0, 0, r] = piece.astype(BF16)

    @pl.when(j < 2)
    def _():
        cos = cos_ref[...]
        sin = sin_ref[...]
        gmat = gmat_ref[...]
        emit(lambda g: _head_norm_rope(acc[:, g * DIL_GROUP_WIDTH:(g + 1) * DIL_GROUP_WIDTH],
                                       gain_ref[:, g * DIL_GROUP_WIDTH:(g + 1) * DIL_GROUP_WIDTH],
                                       cos, sin, gmat))

    @pl.when(j == 2)
    def _():
        emit(lambda g: acc[:, g * DIL_GROUP_WIDTH:(g + 1) * DIL_GROUP_WIDTH])


def _inproj_a(x2, mod3, norm_g, w_a, gain_a, cos, sin, gmat, bsz, seq):
    t, d = x2.shape
    tm = min(1024, seq)
    spt = seq // tm
    n = w_a.shape[1]
    tn = n // 3
    out_shapes, out_specs = [], []
    for _, dil in DIL_GROUPS:
        out_shapes.append(jax.ShapeDtypeStruct((3, bsz, dil, seq // dil, DIL_GROUP_WIDTH), BF16))
        out_specs.append(pl.BlockSpec((1, 1, dil, tm // dil, DIL_GROUP_WIDTH),
                                      lambda i, j: (j, i // spt, 0, i % spt, 0)))
    return pl.pallas_call(
        _inproj_a_kernel,
        out_shape=tuple(out_shapes),
        grid=(t // tm, 3),
        in_specs=[pl.BlockSpec((tm, d), lambda i, j: (i, 0)),
                  pl.BlockSpec((1, 6, d), lambda i, j: (i // spt, 0, 0)),
                  pl.BlockSpec((1, d), lambda i, j: (0, 0)),
                  pl.BlockSpec((d, tn), lambda i, j: (0, j)),
                  pl.BlockSpec((1, tn), lambda i, j: (0, j)),
                  pl.BlockSpec((tm, LANES), lambda i, j: (i, 0)),
                  pl.BlockSpec((tm, LANES), lambda i, j: (i, 0)),
                  pl.BlockSpec((2 * LANES, 2 * LANES), lambda i, j: (0, 0))],
        out_specs=tuple(out_specs),
        scratch_shapes=[pltpu.VMEM((tm, d), BF16),
                        pltpu.VMEM((DIL_GROUP_WIDTH // LANES, tm, LANES), F32)],
        compiler_params=_cparams(2),
        name="inproj_dilated",
    )(x2, mod3, norm_g, w_a, gain_a, cos, sin, gmat)


def _inproj_b_kernel(x_ref, mod_ref, ng_ref, w_ref, gain_ref, bias_ref, cos_ref, sin_ref, gmat_ref,
                     qk_ref, vt_ref, gate_ref, h_sc, *, n_rope_tiles, n_plain_tiles):
    j = pl.program_id(1)

    @pl.when(j == 0)
    def _():
        m = mod_ref[0]
        h_sc[...] = _modulated_norm(x_ref[...], ng_ref[...], m[1:2], m[0:1]).astype(BF16)

    acc = jnp.dot(h_sc[...], w_ref[...], preferred_element_type=F32)
    tn = acc.shape[1]

    @pl.when(j < n_rope_tiles)
    def _():
        cos = cos_ref[...]
        sin = sin_ref[...]
        gmat = gmat_ref[...]
        for p in range(tn // (2 * LANES)):
            sl = slice(p * 2 * LANES, (p + 1) * 2 * LANES)
            qk_ref[:, sl] = _head_norm_rope(acc[:, sl], gain_ref[:, sl], cos, sin, gmat).astype(BF16)

    @pl.when((j >= n_rope_tiles) & (j < n_rope_tiles + n_plain_tiles))
    def _():
        for cc in range(vt_ref.shape[0]):
            vt_ref[cc] = acc[cc * DIFF_TK:(cc + 1) * DIFF_TK, :].T.astype(BF16)

    @pl.when(j >= n_rope_tiles + n_plain_tiles)
    def _():
        gate_ref[...] = jax.nn.sigmoid(acc + bias_ref[...]).astype(BF16)


def _inproj_b(x2, mod3, norm_g, w_b, gain_b, bias_b, cos, sin, gmat, seq, qk_width, v_width):
    t, d = x2.shape
    tm = min(1024, seq)
    spt = seq // tm
    n = w_b.shape[1]
    tn = 512
    n_rope, n_plain = 2 * qk_width // tn, v_width // tn
    n_gate = n // tn - n_rope - n_plain
    kern = functools.partial(_inproj_b_kernel, n_rope_tiles=n_rope, n_plain_tiles=n_plain)
    return pl.pallas_call(
        kern,
        out_shape=(jax.ShapeDtypeStruct((t, n_rope * tn), BF16),
                   jax.ShapeDtypeStruct((t // DIFF_TK, v_width, DIFF_TK), BF16),
                   jax.ShapeDtypeStruct((t, n_gate * tn), BF16)),
        grid=(t // tm, n // tn),
        in_specs=[pl.BlockSpec((tm, d), lambda i, j: (i, 0)),
                  pl.BlockSpec((1, 6, d), lambda i, j: (i // spt, 0, 0)),
                  pl.BlockSpec((1, d), lambda i, j: (0, 0)),
                  pl.BlockSpec((d, tn), lambda i, j: (0, j)),
                  pl.BlockSpec((1, tn), lambda i, j: (0, j)),
                  pl.BlockSpec((1, tn), lambda i, j: (0, j)),
                  pl.BlockSpec((tm, LANES), lambda i, j: (i, 0)),
                  pl.BlockSpec((tm, LANES), lambda i, j: (i, 0)),
                  pl.BlockSpec((2 * LANES, 2 * LANES), lambda i, j: (0, 0))],
        out_specs=(pl.BlockSpec((tm, tn), lambda i, j: (i, jnp.minimum(j, n_rope - 1))),
                   pl.BlockSpec((tm // DIFF_TK, tn, DIFF_TK),
                                lambda i, j: (i, jnp.clip(j - n_rope, 0, n_plain - 1), 0)),
                   pl.BlockSpec((tm, tn), lambda i, j: (i, jnp.clip(j - n_rope - n_plain, 0, n_gate - 1)))),
        scratch_shapes=[pltpu.VMEM((tm, d), BF16)],
        compiler_params=_cparams(2),
        name="inproj_diff_gates",
    )(x2, mod3, norm_g, w_b, gain_b, bias_b, cos, sin, gmat)


def _dilated_attn_kernel(q_ref, k_ref, v_ref, o_ref, lse_ref, *, tq, kw, length, n_sub):
    qi = pl.program_id(1)
    lane_head = lax.broadcasted_iota(jnp.int32, (tq, DIL_GROUP_WIDTH), 1) // HEAD_DIM
    for sub in range(n_sub):
        rows = slice(sub * tq, (sub + 1) * tq)
        q0 = (qi * n_sub + sub) * tq
        kstart = jnp.clip(q0 - DIL_SIDE, 0, length - kw)
        kstart = pl.multiple_of(kstart, DIL_SIDE)
        q = q_ref[0, 0, rows, :]
        k = k_ref[0, 0, pl.ds(kstart, kw), :]
        v = v_ref[0, 0, pl.ds(kstart, kw), :]
        qidx = q0 + lax.broadcasted_iota(jnp.int32, (tq, kw), 0)
        kidx = kstart + lax.broadcasted_iota(jnp.int32, (tq, kw), 1)
        valid = jnp.abs(kidx - qidx) <= DIL_SIDE
        out = jnp.zeros((tq, DIL_GROUP_WIDTH), F32)
        lse_full = jnp.zeros((tq, DIL_GROUP_WIDTH), F32)
        for h in range(DIL_HEADS_PER_GROUP):
            in_head = lane_head == h
            qh = jnp.where(in_head, q, jnp.zeros_like(q))
            s = lax.dot_general(qh, k, _NT, preferred_element_type=F32)
            s = jnp.where(valid, s, NEG_INF)
            m = jnp.max(s, axis=-1, keepdims=True)
            p = jnp.exp(s - m)
            l = jnp.sum(p, axis=-1, keepdims=True)
            oh = jnp.dot(p.astype(BF16), v, preferred_element_type=F32)
            out = jnp.where(in_head, oh / l, out)
            lse_full = jnp.where(in_head, m + jnp.log(l), lse_full)
        o_ref[0, rows, :] = out.astype(BF16)
        lse_ref[0, rows, :] = lse_full


def _dilated_attn(qkv):
    _, bsz, dil, length, w = qkv.shape
    bd = bsz * dil
    qkv = qkv.reshape(3, bd, length, w)
    tq = min(128, length)
    kw = min(tq + 2 * DIL_SIDE, length)
    n_sub = min(4, length // tq)
    tb = tq * n_sub
    kern = functools.partial(_dilated_attn_kernel, tq=tq, kw=kw, length=length, n_sub=n_sub)
    return pl.pallas_call(
        kern,
        out_shape=(jax.ShapeDtypeStruct((bd, length, w), BF16), jax.ShapeDtypeStruct((bd, length, w), F32)),
        grid=(bd, length // tb),
        in_specs=[pl.BlockSpec((1, 1, tb, w), lambda b, i: (0, b, i, 0)),
                  pl.BlockSpec((1, 1, length, w), lambda b, i: (1, b, 0, 0)),
                  pl.BlockSpec((1, 1, length, w), lambda b, i: (2, b, 0, 0))],
        out_specs=(pl.BlockSpec((1, tb, w), lambda b, i: (b, i, 0)),
                   pl.BlockSpec((1, tb, w), lambda b, i: (b, i, 0))),
        compiler_params=_cparams(2),
        name="dilated_attn",
    )(qkv, qkv, qkv)


def _dil_combine_kernel(o0_ref, l0_ref, o1_ref, l1_ref, o2_ref, l2_ref, oa_ref, nat_sc):
    tm = oa_ref.shape[0]
    n_half = DIL_GROUP_WIDTH // LANES

    def natural(ref, dil):
        if dil == 1:
            return ref[0, 0].astype(F32)
        rows = tm // dil
        for r in range(dil):
            blk = ref[0, r].astype(F32)
            for hf in range(n_half):
                nat_sc[hf, pl.ds(r, rows, stride=dil), :] = blk[:, hf * LANES:(hf + 1) * LANES]
        return jnp.concatenate([nat_sc[hf] for hf in range(n_half)], axis=1)

    dils = [d for _, d in DIL_GROUPS]
    lses = [natural(r, d) for r, d in zip((l0_ref, l1_ref, l2_ref), dils)]
    m = jnp.maximum(jnp.maximum(lses[0], lses[1]), lses[2])
    ws = [jnp.exp(l - m) for l in lses]
    den = ws[0] + ws[1] + ws[2]
    num = jnp.zeros_like(den)
    for w, r, d in zip(ws, (o0_ref, o1_ref, o2_ref), dils):
        num = num + w * natural(r, d)
    oa_ref[...] = (num / den).astype(BF16)


def _dil_combine(outs, lses, bsz, seq):
    tm = min(512, seq)
    spt = seq // tm
    in_specs, args = [], []
    for (o, l), (_, dil) in zip(zip(outs, lses), DIL_GROUPS):
        shape4 = (bsz, dil, seq // dil, DIL_GROUP_WIDTH)
        spec = pl.BlockSpec((1, dil, tm // dil, DIL_GROUP_WIDTH), lambda i: (i // spt, 0, i % spt, 0))
        in_specs += [spec, spec]
        args += [o.reshape(shape4), l.reshape(shape4)]
    return pl.pallas_call(
        _dil_combine_kernel,
        out_shape=jax.ShapeDtypeStruct((bsz * seq, DIL_GROUP_WIDTH), BF16),
        grid=(bsz * spt,),
        in_specs=in_specs,
        out_specs=pl.BlockSpec((tm, DIL_GROUP_WIDTH), lambda i: (i, 0)),
        scratch_shapes=[pltpu.VMEM((DIL_GROUP_WIDTH // LANES, tm, LANES), F32)],
        compiler_params=_cparams(1),
        name="dilated_combine",
    )(*args)


def _diff_attn_kernel(bounded_ref, q_ref, k_ref, vt_ref, lq1_ref, lk1_ref, lq2_ref, lk2_ref, sg_ref, o_ref,
                      l_sc, acc_sc, *, lam_init):
    tq = q_ref.shape[0]
    n_chunks = vt_ref.shape[0]
    q = q_ref[...]
    lane = lax.broadcasted_iota(jnp.int32, q.shape, 1)
    zero = jnp.zeros_like(q)
    qz = jnp.concatenate([jnp.where(lane < HEAD_DIM, q, zero), jnp.where(lane >= HEAD_DIM, q, zero)], axis=0)

    def scores(c):
        ks = pl.multiple_of(c * DIFF_TK, DIFF_TK)
        k = k_ref[pl.ds(ks, DIFF_TK), :]
        return lax.dot_general(k, qz, _NT, preferred_element_type=F32)

    @pl.when(bounded_ref[0] != 0)
    def _():
        l = jnp.zeros((1, 2 * tq), F32)
        acc = jnp.zeros((vt_ref.shape[1], 2 * tq), F32)
        for c in range(n_chunks):
            p = jnp.exp2(scores(c))
            l = l + jnp.sum(p, axis=0, keepdims=True)
            acc = acc + jnp.dot(vt_ref[c], p.astype(BF16), preferred_element_type=F32)
        l_sc[...] = l
        acc_sc[...] = acc

    @pl.when(bounded_ref[0] == 0)
    def _():
        def body(c, carry):
            m, l, acc = carry
            st = scores(c)
            m_new = jnp.maximum(m, jnp.max(st, axis=0, keepdims=True))
            alpha = jnp.exp2(m - m_new)
            p = jnp.exp2(st - m_new)
            l = alpha * l + jnp.sum(p, axis=0, keepdims=True)
            acc = alpha * acc + jnp.dot(vt_ref[c], p.astype(BF16), preferred_element_type=F32)
            return m_new, l, acc

        init = (jnp.full((1, 2 * tq), -jnp.inf, F32), jnp.zeros((1, 2 * tq), F32),
                jnp.zeros((vt_ref.shape[1], 2 * tq), F32))
        _, l, acc = lax.fori_loop(0, n_chunks, body, init)
        l_sc[...] = l
        acc_sc[...] = acc

    o = acc_sc[...] / l_sc[...]
    lam = (jnp.exp(jnp.sum(lq1_ref[...] * lk1_ref[...], axis=-1, keepdims=True))
           - jnp.exp(jnp.sum(lq2_ref[...] * lk2_ref[...], axis=-1, keepdims=True)) + lam_init)
    obt = o[:, :tq] - lam * o[:, tq:]
    ms = jnp.mean(obt * obt, axis=0, keepdims=True)
    obn = obt * lax.rsqrt(ms + EPS)
    o_ref[...] = (obn.T * (sg_ref[...] * (1.0 - lam_init))).astype(BF16)


def _diff_attn(qk, vt, score_bounded, lam_vecs, subln_g, bsz, seq, lam_init):
    t = qk.shape[0]
    tq = min(256, seq)
    qpt = seq // tq
    cps = seq // DIFF_TK
    hw = 2 * HEAD_DIM
    kern = functools.partial(_diff_attn_kernel, lam_init=lam_init)
    vec_spec = pl.BlockSpec((1, HEAD_DIM), lambda b, h, i: (0, 0))
    return pl.pallas_call(
        kern,
        out_shape=jax.ShapeDtypeStruct((t, DIFF_HEADS * hw), BF16),
        grid=(bsz, DIFF_HEADS, qpt),
        in_specs=[pl.BlockSpec(memory_space=pltpu.SMEM),
                  pl.BlockSpec((tq, hw), lambda b, h, i: (b * qpt + i, h)),
                  pl.BlockSpec((seq, hw), lambda b, h, i: (b, DIFF_HEADS + h)),
                  pl.BlockSpec((cps, hw, DIFF_TK), lambda b, h, i: (b, h, 0)),
                  vec_spec, vec_spec, vec_spec, vec_spec,
                  pl.BlockSpec((1, hw), lambda b, h, i: (0, 0))],
        out_specs=pl.BlockSpec((tq, hw), lambda b, h, i: (b * qpt + i, h)),
        scratch_shapes=[pltpu.VMEM((1, 2 * tq), F32), pltpu.VMEM((hw, 2 * tq), F32)],
        compiler_params=_cparams(3),
        name="diff_attn",
    )(score_bounded, qk, qk, vt, *lam_vecs, subln_g)


def _merge_kernel(oa_ref, ob_ref, ga_ref, gb_ref, x_ref, mod_ref, ng_ref, wpa_ref, wpb_ref, wo_ref,
                  x1_ref, h2_ref):
    ba = jnp.dot(oa_ref[...], wpa_ref[...], preferred_element_type=F32)
    bb = jnp.dot(ob_ref[...], wpb_ref[...], preferred_element_type=F32)
    mixed = ga_ref[...].astype(F32) * ba + gb_ref[...].astype(F32) * bb
    mo = jnp.dot(mixed.astype(BF16), wo_ref[...], preferred_element_type=F32)
    m = mod_ref[0]
    x1 = x_ref[...] + m[2:3] * mo
    x1_ref[...] = x1
    h2_ref[...] = _modulated_norm(x1, ng_ref[...], m[4:5], m[3:4]).astype(BF16)


def _merge(oa, ob, gates, x2, mod3, norm2_g, wpa, wpb, wo, seq):
    t, d = x2.shape
    tm = min(512, seq)
    spt = seq // tm
    full = lambda i: (0, 0)
    return pl.pallas_call(
        _merge_kernel,
        out_shape=(jax.ShapeDtypeStruct((t, d), F32), jax.ShapeDtypeStruct((t, d), BF16)),
        grid=(t // tm,),
        in_specs=[pl.BlockSpec((tm, oa.shape[1]), lambda i: (i, 0)),
                  pl.BlockSpec((tm, d), lambda i: (i, 0)),
                  pl.BlockSpec((tm, d), lambda i: (i, 0)),
                  pl.BlockSpec((tm, d), lambda i: (i, 1)),
                  pl.BlockSpec((tm, d), lambda i: (i, 0)),
                  pl.BlockSpec((1, 6, d), lambda i: (i // spt, 0, 0)),
                  pl.BlockSpec((1, d), full),
                  pl.BlockSpec(wpa.shape, full),
                  pl.BlockSpec(wpb.shape, full),
                  pl.BlockSpec(wo.shape, full)],
        out_specs=(pl.BlockSpec((tm, d), lambda i: (i, 0)), pl.BlockSpec((tm, d), lambda i: (i, 0))),
        compiler_params=_cparams(1),
        name="merge_outproj_norm2",
    )(oa, ob, gates, gates, x2, mod3, norm2_g, wpa, wpb, wo)


def _extract_topk(s, idx_f, val_sc, idx_sc):
    for k in range(PEER_TOPK):
        m = jnp.max(s, axis=0, keepdims=True)
        idx = jnp.min(jnp.where(s == m, idx_f, 1e9), axis=0, keepdims=True)
        val_sc[k:k + 1, :] = m
        idx_sc[k:k + 1, :] = idx
        s = jnp.where(idx_f == idx, -jnp.inf, s)


def _peer_topk_kernel(h_ref, wq_ref, sk_ref, e1_ref, e2_ref, g_ref,
                      q_sc, v1_sc, i1_sc, v2_sc, i2_sc, top_sc, lab_sc, e1t_sc, e2t_sc, gt_sc):
    tm = h_ref.shape[0]
    q = jnp.dot(h_ref[...], wq_ref[...], preferred_element_type=F32).astype(BF16)
    for piece in range(2 * PEER_HEADS):
        q_sc[piece] = q[:, piece * PEER_HALF:(piece + 1) * PEER_HALF]
    key_idx = lax.broadcasted_iota(jnp.int32, (PEER_N_KEYS, tm), 0).astype(F32)
    sub8 = lax.broadcasted_iota(jnp.int32, (8, tm), 0).astype(F32)
    neg = jnp.full((8, tm), -jnp.inf, F32)

    def head(h, carry):
        for side, (v_sc, i_sc) in enumerate(((v1_sc, i1_sc), (v2_sc, i2_sc))):
            qs = q_sc[2 * h + side]
            st = lax.dot_general(sk_ref[2 * h + side], qs, _NT, preferred_element_type=F32)
            _extract_topk(st, key_idx, v_sc, i_sc)
        v2lo = v2_sc[0:8, :]
        cands = [v1_sc[0:1, :] + v2lo, v1_sc[0:1, :] + v2_sc[8:16, :]]
        labels = [sub8, sub8 + 8.0]
        for a in range(1, 8):
            nb = PEER_TOPK // (a + 1)
            cands.append(jnp.where(sub8 < nb, v1_sc[a:a + 1, :] + v2lo, neg))
            labels.append(sub8 + float(a * PEER_TOPK))
        cands.append(v1_sc[8:16, :] + v2_sc[0:1, :])
        labels.append((sub8 + 8.0) * float(PEER_TOPK))
        cand = jnp.concatenate(cands, axis=0)
        label = jnp.concatenate(labels, axis=0)
        _extract_topk(cand, label, top_sc, lab_sc)
        top = top_sc[...]
        lab = lab_sc[...]
        a_sel = jnp.floor(lab * (1.0 / PEER_TOPK))
        b_sel = lab - a_sel * PEER_TOPK
        e1 = jnp.zeros_like(top)
        e2 = jnp.zeros_like(top)
        for r in range(PEER_TOPK):
            e1 = jnp.where(a_sel == float(r), i1_sc[r:r + 1, :], e1)
            e2 = jnp.where(b_sel == float(r), i2_sc[r:r + 1, :], e2)
        p = jnp.exp(top - jnp.max(top, axis=0, keepdims=True))
        gate = p / jnp.sum(p, axis=0, keepdims=True)
        row = pl.multiple_of(h * PEER_TOPK, PEER_TOPK)
        e1t_sc[pl.ds(row, PEER_TOPK), :] = e1
        e2t_sc[pl.ds(row, PEER_TOPK), :] = e2
        gt_sc[pl.ds(row, PEER_TOPK), :] = gate
        return carry

    lax.fori_loop(0, PEER_HEADS, head, 0)
    e1_ref[...] = e1t_sc[...].T
    e2_ref[...] = e2t_sc[...].T
    g_ref[...] = gt_sc[...].T


def _peer_topk(h2, wq, sk):
    t, d = h2.shape
    tm = min(512, t)
    nsel = PEER_HEADS * PEER_TOPK
    out = jax.ShapeDtypeStruct((t, nsel), F32)
    spec = pl.BlockSpec((tm, nsel), lambda i: (i, 0))
    small = lambda: pltpu.VMEM((PEER_TOPK, tm), F32)
    big = lambda: pltpu.VMEM((nsel, tm), F32)
    return pl.pallas_call(
        _peer_topk_kernel,
        out_shape=(out, out, out),
        grid=(t // tm,),
        in_specs=[pl.BlockSpec((tm, d), lambda i: (i, 0)),
                  pl.BlockSpec(wq.shape, lambda i: (0, 0)),
                  pl.BlockSpec(sk.shape, lambda i: (0, 0, 0))],
        out_specs=(spec, spec, spec),
        scratch_shapes=[pltpu.VMEM((2 * PEER_HEADS, tm, PEER_HALF), BF16),
                        small(), small(), small(), small(), small(), small(), big(), big(), big()],
        compiler_params=_cparams(1),
        name="peer_topk",
    )(h2, wq, sk)


def _peer_gates_kernel(e1_ref, e2_ref, g_ref, o_ref, g_sc, *, pitch):
    tm = e1_ref.shape[0]
    row_idx = lax.broadcasted_iota(jnp.int32, (PEER_N_KEYS, e1_ref.shape[1]), 0).astype(F32)

    def token(t, carry):
        r1 = e1_ref[pl.ds(t, 1), :]
        r2 = e2_ref[pl.ds(t, 1), :]
        gt = g_ref[pl.ds(t, 1), :]
        sel_i = jnp.where(row_idx == r1, 1.0, 0.0).astype(BF16)
        sel_j = jnp.where(row_idx == r2, gt, 0.0).astype(BF16)
        gm = lax.dot_general(sel_i, sel_j, _NT, preferred_element_type=F32)
        g_sc[pl.ds(t, PEER_N_KEYS, stride=pitch), :] = gm
        return carry

    lax.fori_loop(0, tm, token, 0, unroll=16)

    def emit(i, carry):
        start = pl.multiple_of(i * pitch, 8)
        o_ref[i] = g_sc[pl.ds(start, tm), :].astype(BF16)
        return carry

    lax.fori_loop(0, PEER_N_KEYS, emit, 0)


def _peer_gates(e1, e2, gate):
    t, nsel = e1.shape
    tm = min(128, t)
    pitch = tm + 8
    spec = pl.BlockSpec((tm, nsel), lambda i: (i, 0))
    return pl.pallas_call(
        functools.partial(_peer_gates_kernel, pitch=pitch),
        out_shape=jax.ShapeDtypeStruct((PEER_N_KEYS, t, PEER_N_KEYS), BF16),
        grid=(t // tm,),
        in_specs=[spec, spec, spec],
        out_specs=pl.BlockSpec((PEER_N_KEYS, tm, PEER_N_KEYS), lambda i: (0, i, 0)),
        scratch_shapes=[pltpu.VMEM((PEER_N_KEYS * pitch, PEER_N_KEYS), F32)],
        compiler_params=_cparams(1),
        name="peer_gate_matrix",
    )(e1, e2, gate)


def _peer_dense_kernel(h_ref, u_ref, v_ref, g_ref, x1_ref, mod_ref, o_ref, acc_sc):
    c = pl.program_id(1)

    @pl.when(c == 0)
    def _():
        acc_sc[...] = jnp.zeros_like(acc_sc)

    a = lax.dot_general(h_ref[...], u_ref[...], _NT, preferred_element_type=F32)
    act = 0.5 * a * (1.0 + lax.erf(a * (1.0 / math.sqrt(2.0))))
    ws = [(act[:, ib * LANES:(ib + 1) * LANES] * g_ref[ib].astype(F32)).astype(BF16)
          for ib in range(g_ref.shape[0])]
    w = jnp.concatenate(ws, axis=1)
    acc_sc[...] += jnp.dot(w, v_ref[...], preferred_element_type=F32)

    @pl.when(c == pl.num_programs(1) - 1)
    def _():
        o_ref[...] = x1_ref[...] + mod_ref[0][5:6] * acc_sc[...]


def _peer_dense(h2, u, v, gmat, x1, mod3, seq):
    t, d = h2.shape
    n_exp = u.shape[0]
    tm = min(1024, seq)
    spt = seq // tm
    tn = 512
    return pl.pallas_call(
        _peer_dense_kernel,
        out_shape=jax.ShapeDtypeStruct((t, d), F32),
        grid=(t // tm, n_exp // tn),
        in_specs=[pl.BlockSpec((tm, d), lambda i, c: (i, 0)),
                  pl.BlockSpec((tn, d), lambda i, c: (c, 0)),
                  pl.BlockSpec((tn, d), lambda i, c: (c, 0)),
                  pl.BlockSpec((tn // PEER_N_KEYS, tm, PEER_N_KEYS), lambda i, c: (c, i, 0)),
                  pl.BlockSpec((tm, d), lambda i, c: (i, 0)),
                  pl.BlockSpec((1, 6, d), lambda i, c: (i // spt, 0, 0))],
        out_specs=pl.BlockSpec((tm, d), lambda i, c: (i, 0)),
        scratch_shapes=[pltpu.VMEM((tm, d), F32)],
        compiler_params=_cparams(2),
        name="peer_dense",
    )(h2, u, v, gmat, x1, mod3)


def _layer(x, c, cos, sin, lam_init, w_ada, b_ada, norm1_g, w_in, b_gate, qn_a, kn_a, w_proj_a,
           qn_b, kn_b, lam_q1, lam_k1, lam_q2, lam_k2, subln_g, w_proj_b, w_out, norm2_g,
           w_query, sub_keys, expert_u, expert_v):
    bsz, seq, d = x.shape
    t = bsz * seq
    x2 = x.reshape(t, d)
    mod3 = _ada(c, w_ada, b_ada).reshape(bsz, 6, d)
    gmat = _group_sum_matrix()
    scale = HEAD_DIM ** -0.5

    dil_w = len(DIL_GROUPS) * DIL_GROUP_WIDTH
    diff_w = DIFF_HEADS * 2 * HEAD_DIM
    a_cols = 3 * dil_w
    w_in_bf = w_in.astype(BF16)
    ones = lambda n: jnp.ones((n,), F32)
    gain_a = jnp.concatenate([jnp.tile(qn_a, dil_w // HEAD_DIM) * scale, jnp.tile(kn_a, dil_w // HEAD_DIM),
                              ones(dil_w)]).reshape(1, a_cols)
    qkv_groups = _inproj_a(x2, mod3, norm1_g.reshape(1, d), w_in_bf[:, :a_cols], gain_a, cos, sin, gmat, bsz, seq)

    n_b = w_in.shape[1] - a_cols
    gain_b = jnp.concatenate([jnp.tile(qn_b, diff_w // HEAD_DIM) * (scale * LOG2E),
                              jnp.tile(kn_b, diff_w // HEAD_DIM), ones(n_b - 2 * diff_w)]).reshape(1, n_b)
    bias_b = jnp.concatenate([jnp.zeros((3 * diff_w,), F32), b_gate]).reshape(1, n_b)
    qk_b, vt_b, gates = _inproj_b(x2, mod3, norm1_g.reshape(1, d), w_in_bf[:, a_cols:], gain_b, bias_b,
                                  cos, sin, gmat, seq, diff_w, diff_w)

    outs, lses = [], []
    for qkv in qkv_groups:
        o, l = _dilated_attn(qkv)
        outs.append(o)
        lses.append(l)
    oa = _dil_combine(outs, lses, bsz, seq)

    lam_vecs = [v.reshape(1, HEAD_DIM) for v in (lam_q1, lam_k1, lam_q2, lam_k2)]
    score_cap = (HEAD_DIM * scale * LOG2E * 1.01) * jnp.max(jnp.abs(qn_b)) * jnp.max(jnp.abs(kn_b))
    score_bounded = (score_cap <= DIFF_SCORE_BOUND).astype(jnp.int32).reshape(1)
    ob = _diff_attn(qk_b, vt_b, score_bounded, lam_vecs, subln_g.reshape(1, 2 * HEAD_DIM), bsz, seq, lam_init)

    x1, h2 = _merge(oa, ob, gates, x2, mod3, norm2_g.reshape(1, d), w_proj_a.astype(BF16),
                    w_proj_b.astype(BF16), w_out.astype(BF16), seq)

    sk = sub_keys.astype(BF16).reshape(PEER_HEADS * 2, PEER_N_KEYS, PEER_HALF)
    e1, e2, gate = _peer_topk(h2, w_query.astype(BF16), sk)
    gdense = _peer_gates(e1, e2, gate)
    out = _peer_dense(h2, expert_u.astype(BF16), expert_v.astype(BF16), gdense, x1, mod3, seq)
    return out.reshape(bsz, seq, d)


def kernel(x, c, positions, w_ada, b_ada, norm1_g, w_in, b_gate, qn_a, kn_a, w_proj_a, qn_b, kn_b,
           lam_q1, lam_k1, lam_q2, lam_k2, subln_g, w_proj_b, w_out, norm2_g, w_query, sub_keys,
           expert_u, expert_v):
    cos, sin = _rope_tables(positions)
    for l in range(w_ada.shape[0]):
        lam_init = 0.8 - 0.6 * math.exp(-0.3 * l)
        x = _layer(x, c, cos, sin, lam_init, w_ada[l], b_ada[l], norm1_g[l], w_in[l], b_gate[l],
                   qn_a[l], kn_a[l], w_proj_a[l], qn_b[l], kn_b[l], lam_q1[l], lam_k1[l], lam_q2[l],
                   lam_k2[l], subln_g[l], w_proj_b[l], w_out[l], norm2_g[l], w_query[l], sub_keys[l],
                   expert_u[l], expert_v[l])
    return x
```

```python
import functools
import math

import jax
import jax.numpy as jnp
from jax import lax
from jax.experimental import pallas as pl
from jax.experimental.pallas import tpu as pltpu

F32 = jnp.float32
BF16 = jnp.bfloat16

HEAD_DIM = 64
ROPE_THETA = 10000.0
EPS = 1e-6
NEG_INF = -1e30

DIL_GROUPS = ((128, 1), (512, 4), (2048, 16))
DIL_HEADS_PER_GROUP = 4
DIL_GROUP_WIDTH = DIL_HEADS_PER_GROUP * HEAD_DIM
DIL_SIDE = 64

DIFF_HEADS = 8
DIFF_TK = 512
LOG2E = 1.4426950408889634
DIFF_SCORE_BOUND = 60.0
PEER_HEADS = 8
PEER_N_KEYS = 128
PEER_TOPK = 16
PEER_HALF = 128

LANES = 128
VMEM_LIMIT = 56 * 1024 * 1024

_NT = (((1,), (1,)), ((), ()))


def _cparams(n_axes):
    return pltpu.CompilerParams(dimension_semantics=("arbitrary",) * n_axes,
                                vmem_limit_bytes=VMEM_LIMIT)


def _ada_kernel(c_ref, w_ref, b_ref, o_ref):
    c = c_ref[...]
    sc = (c * jax.nn.sigmoid(c)).astype(BF16)
    o_ref[...] = jnp.dot(sc, w_ref[...].astype(BF16), preferred_element_type=F32) + b_ref[...]


def _ada(c, w_ada, b_ada):
    bsz, d = c.shape
    n = w_ada.shape[1]
    tn = 1024
    return pl.pallas_call(
        _ada_kernel,
        out_shape=jax.ShapeDtypeStruct((bsz, n), F32),
        grid=(n // tn,),
        in_specs=[pl.BlockSpec((bsz, d), lambda j: (0, 0)),
                  pl.BlockSpec((d, tn), lambda j: (0, j)),
                  pl.BlockSpec((1, tn), lambda j: (0, j))],
        out_specs=pl.BlockSpec((bsz, tn), lambda j: (0, j)),
        compiler_params=_cparams(1),
        name="ada_mod",
    )(c, w_ada, b_ada.reshape(1, n))


def _rope_kernel(pos_ref, freq_ref, cos_ref, sin_ref):
    ang = pos_ref[...] * freq_ref[...]
    lane = lax.broadcasted_iota(jnp.int32, ang.shape, 1)
    first_half = (lane % HEAD_DIM) < (HEAD_DIM // 2)
    cos_ref[...] = jnp.cos(ang)
    s = jnp.sin(ang)
    sin_ref[...] = jnp.where(first_half, -s, s)


def _rope_tables(positions):
    t = positions.size
    tm = min(1024, t)
    pos = positions.reshape(t, 1).astype(F32)
    inv_freq = 1.0 / (ROPE_THETA ** (jnp.arange(0, HEAD_DIM, 2, dtype=F32) / HEAD_DIM))
    freq = jnp.tile(inv_freq, LANES // (HEAD_DIM // 2)).reshape(1, LANES)
    return pl.pallas_call(
        _rope_kernel,
        out_shape=(jax.ShapeDtypeStruct((t, LANES), F32), jax.ShapeDtypeStruct((t, LANES), F32)),
        grid=(t // tm,),
        in_specs=[pl.BlockSpec((tm, 1), lambda i: (i, 0)),
                  pl.BlockSpec((1, LANES), lambda i: (0, 0))],
        out_specs=(pl.BlockSpec((tm, LANES), lambda i: (i, 0)),
                   pl.BlockSpec((tm, LANES), lambda i: (i, 0))),
        compiler_params=_cparams(1),
        name="rope_tables",
    )(pos, freq)


def _modulated_norm(x, g, scale, shift):
    ms = jnp.mean(x * x, axis=-1, keepdims=True)
    return (x * lax.rsqrt(ms + EPS)) * g * (1.0 + scale) + shift


def _head_norm_rope(acc, gain, cos, sin, gmat):
    ss = jnp.dot((acc * acc).astype(BF16), gmat, preferred_element_type=F32)
    yn = acc * lax.rsqrt(ss * (1.0 / HEAD_DIM) + EPS) * gain
    lane = lax.broadcasted_iota(jnp.int32, cos.shape, 1)
    first_half = (lane % HEAD_DIM) < (HEAD_DIM // 2)
    outs = []
    for hf in range(acc.shape[1] // LANES):
        y = yn[:, hf * LANES:(hf + 1) * LANES]
        up = pltpu.roll(y, HEAD_DIM // 2, axis=1)
        down = pltpu.roll(y, LANES - HEAD_DIM // 2, axis=1)
        swapped = jnp.where(first_half, down, up)
        outs.append(y * cos + swapped * sin)
    return jnp.concatenate(outs, axis=1)


def _group_sum_matrix():
    r = lax.broadcasted_iota(jnp.int32, (2 * LANES, 2 * LANES), 0) // HEAD_DIM
    c = lax.broadcasted_iota(jnp.int32, (2 * LANES, 2 * LANES), 1) // HEAD_DIM
    return (r == c).astype(BF16)


def _inproj_a_kernel(x_ref, mod_ref, ng_ref, w_ref, gain_ref, cos_ref, sin_ref, gmat_ref,
                     o0_ref, o1_ref, o2_ref, h_sc, y_sc):
    j = pl.program_id(1)
    tm = x_ref.shape[0]

    @pl.when(j == 0)
    def _():
        m = mod_ref[0]
        h_sc[...] = _modulated_norm(x_ref[...], ng_ref[...], m[1:2], m[0:1]).astype(BF16)

    acc = jnp.dot(h_sc[...], w_ref[...], preferred_element_type=F32)
    outs = (o0_ref, o1_ref, o2_ref)

    def emit(get_piece):
        for g, (_, dil) in enumerate(DIL_GROUPS):
            y = get_piece(g)
            if dil == 1:
                outs[g][0, 0, 0] = y.astype(BF16)
            else:
                rows = tm // dil
                for hf in range(DIL_GROUP_WIDTH // LANES):
                    y_sc[hf] = y[:, hf * LANES:(hf + 1) * LANES]
                for r in range(dil):
                    piece = jnp.concatenate(
                        [y_sc[hf, pl.ds(r, rows, stride=dil), :] for hf in range(DIL_GROUP_WIDTH // LANES)],
                        axis=1)
                    outs[g][0, 0, r] = piece.astype(BF16)

    @pl.when(j < 2)
    def _():
        cos = cos_ref[...]
        sin = sin_ref[...]
        gmat = gmat_ref[...]
        emit(lambda g: _head_norm_rope(acc[:, g * DIL_GROUP_WIDTH:(g + 1) * DIL_GROUP_WIDTH],
                                       gain_ref[:, g * DIL_GROUP_WIDTH:(g + 1) * DIL_GROUP_WIDTH],
                                       cos, sin, gmat))

    @pl.when(j == 2)
    def _():
        emit(lambda g: acc[:, g * DIL_GROUP_WIDTH:(g + 1) * DIL_GROUP_WIDTH])


def _inproj_a(x2, mod3, norm_g, w_a, gain_a, cos, sin, gmat, bsz, seq):
    t, d = x2.shape
    tm = min(1024, seq)
    spt = seq // tm
    n = w_a.shape[1]
    tn = n // 3
    out_shapes, out_specs = [], []
    for _, dil in DIL_GROUPS:
        out_shapes.append(jax.ShapeDtypeStruct((3, bsz, dil, seq // dil, DIL_GROUP_WIDTH), BF16))
        out_specs.append(pl.BlockSpec((1, 1, dil, tm // dil, DIL_GROUP_WIDTH),
                                      lambda i, j: (j, i // spt, 0, i % spt, 0)))
    return pl.pallas_call(
        _inproj_a_kernel,
        out_shape=tuple(out_shapes),
        grid=(t // tm, 3),
        in_specs=[pl.BlockSpec((tm, d), lambda i, j: (i, 0)),
                  pl.BlockSpec((1, 6, d), lambda i, j: (i // spt, 0, 0)),
                  pl.BlockSpec((1, d), lambda i, j: (0, 0)),
                  pl.BlockSpec((d, tn), lambda i, j: (0, j)),
                  pl.BlockSpec((1, tn), lambda i, j: (0, j)),
                  pl.BlockSpec((tm, LANES), lambda i, j: (i, 0)),
                  pl.BlockSpec((tm, LANES), lambda i, j: (i, 0)),
                  pl.BlockSpec((2 * LANES, 2 * LANES), lambda i, j: (0, 0))],
        out_specs=tuple(out_specs),
        scratch_shapes=[pltpu.VMEM((tm, d), BF16),
                        pltpu.VMEM((DIL_GROUP_WIDTH // LANES, tm, LANES), F32)],
        compiler_params=_cparams(2),
        name="inproj_dilated",
    )(x2, mod3, norm_g, w_a, gain_a, cos, sin, gmat)


def _inproj_b_kernel(x_ref, mod_ref, ng_ref, w_ref, gain_ref, bias_ref, cos_ref, sin_ref, gmat_ref,
                     qk_ref, vt_ref, gate_ref, h_sc, *, n_rope_tiles, n_plain_tiles):
    j = pl.program_id(1)

    @pl.when(j == 0)
    def _():
        m = mod_ref[0]
        h_sc[...] = _modulated_norm(x_ref[...], ng_ref[...], m[1:2], m[0:1]).astype(BF16)

    acc = jnp.dot(h_sc[...], w_ref[...], preferred_element_type=F32)
    tn = acc.shape[1]

    @pl.when(j < n_rope_tiles)
    def _():
        cos = cos_ref[...]
        sin = sin_ref[...]
        gmat = gmat_ref[...]
        for p in range(tn // (2 * LANES)):
            sl = slice(p * 2 * LANES, (p + 1) * 2 * LANES)
            qk_ref[:, sl] = _head_norm_rope(acc[:, sl], gain_ref[:, sl], cos, sin, gmat).astype(BF16)

    @pl.when((j >= n_rope_tiles) & (j < n_rope_tiles + n_plain_tiles))
    def _():
        for cc in range(vt_ref.shape[0]):
            vt_ref[cc] = acc[cc * DIFF_TK:(cc + 1) * DIFF_TK, :].T.astype(BF16)

    @pl.when(j >= n_rope_tiles + n_plain_tiles)
    def _():
        gate_ref[...] = jax.nn.sigmoid(acc + bias_ref[...]).astype(BF16)


def _inproj_b(x2, mod3, norm_g, w_b, gain_b, bias_b, cos, sin, gmat, seq, qk_width, v_width):
    t, d = x2.shape
    tm = min(1024, seq)
    spt = seq // tm
    n = w_b.shape[1]
    tn = 512
    n_rope, n_plain = 2 * qk_width // tn, v_width // tn
    n_gate = n // tn - n_rope - n_plain
    kern = functools.partial(_inproj_b_kernel, n_rope_tiles=n_rope, n_plain_tiles=n_plain)
    return pl.pallas_call(
        kern,
        out_shape=(jax.ShapeDtypeStruct((t, n_rope * tn), BF16),
                   jax.ShapeDtypeStruct((t // DIFF_TK, v_width, DIFF_TK), BF16),
                   jax.ShapeDtypeStruct((t, n_gate * tn), BF16)),
        grid=(t // tm, n // tn),
        in_specs=[pl.BlockSpec((tm, d), lambda i, j: (i, 0)),
                  pl.BlockSpec((1, 6, d), lambda i, j: (i // spt, 0, 0)),
                  pl.BlockSpec((1, d), lambda i, j: (0, 0)),
                  pl.BlockSpec((d, tn), lambda i, j: (0, j)),
                  pl.BlockSpec((1, tn), lambda i, j: (0, j)),
                  pl.BlockSpec((1, tn), lambda i, j: (0, j)),
                  pl.BlockSpec((tm, LANES), lambda i, j: (i, 0)),
                  pl.BlockSpec((tm, LANES), lambda i, j: (i, 0)),
                  pl.BlockSpec((2 * LANES, 2 * LANES), lambda i, j: (0, 0))],
        out_specs=(pl.BlockSpec((tm, tn), lambda i, j: (i, jnp.minimum(j, n_rope - 1))),
                   pl.BlockSpec((tm // DIFF_TK, tn, DIFF_TK),
                                lambda i, j: (i, jnp.clip(j - n_rope, 0, n_plain - 1), 0)),
                   pl.BlockSpec((tm, tn), lambda i, j: (i, jnp.clip(j - n_rope - n_plain, 0, n_gate - 1)))),
        scratch_shapes=[pltpu.VMEM((tm, d), BF16)],
        compiler_params=_cparams(2),
        name="inproj_diff_gates",
    )(x2, mod3, norm_g, w_b, gain_b, bias_b, cos, sin, gmat)


def _dilated_attn_kernel(q_ref, k_ref, v_ref, o_ref, lse_ref, *, tq, kw, length, n_sub):
    qi = pl.program_id(1)
    lane_head = lax.broadcasted_iota(jnp.int32, (tq, DIL_GROUP_WIDTH), 1) // HEAD_DIM
    for sub in range(n_sub):
        rows = slice(sub * tq, (sub + 1) * tq)
        q0 = (qi * n_sub + sub) * tq
        kstart = jnp.clip(q0 - DIL_SIDE, 0, length - kw)
        kstart = pl.multiple_of(kstart, DIL_SIDE)
        q = q_ref[0, 0, rows, :]
        k = k_ref[0, 0, pl.ds(kstart, kw), :]
        v = v_ref[0, 0, pl.ds(kstart, kw), :]
        qidx = q0 + lax.broadcasted_iota(jnp.int32, (tq, kw), 0)
        kidx = kstart + lax.broadcasted_iota(jnp.int32, (tq, kw), 1)
        valid = jnp.abs(kidx - qidx) <= DIL_SIDE
        out = jnp.zeros((tq, DIL_GROUP_WIDTH), F32)
        lse_full = jnp.zeros((tq, DIL_GROUP_WIDTH), F32)
        for h in range(DIL_HEADS_PER_GROUP):
            in_head = lane_head == h
            qh = jnp.where(in_head, q, jnp.zeros_like(q))
            s = lax.dot_general(qh, k, _NT, preferred_element_type=F32)
            s = jnp.where(valid, s, NEG_INF)
            m = jnp.max(s, axis=-1, keepdims=True)
            p = jnp.exp(s - m)
            l = jnp.sum(p, axis=-1, keepdims=True)
            oh = jnp.dot(p.astype(BF16), v, preferred_element_type=F32)
            out = jnp.where(in_head, oh / l, out)
            lse_full = jnp.where(in_head, m + jnp.log(l), lse_full)
        o_ref[0, rows, :] = out.astype(BF16)
        lse_ref[0, rows, :] = lse_full


def _dilated_attn(qkv):
    _, bsz, dil, length, w = qkv.shape
    bd = bsz * dil
    qkv = qkv.reshape(3, bd, length, w)
    tq = min(128, length)
    kw = min(tq + 2 * DIL_SIDE, length)
    n_sub = min(4, length // tq)
    tb = tq * n_sub
    kern = functools.partial(_dilated_attn_kernel, tq=tq, kw=kw, length=length, n_sub=n_sub)
    return pl.pallas_call(
        kern,
        out_shape=(jax.ShapeDtypeStruct((bd, length, w), BF16), jax.ShapeDtypeStruct((bd, length, w), F32)),
        grid=(bd, length // tb),
        in_specs=[pl.BlockSpec((1, 1, tb, w), lambda b, i: (0, b, i, 0)),
                  pl.BlockSpec((1, 1, length, w), lambda b, i: (1, b, 0, 0)),
                  pl.BlockSpec((1, 1, length, w), lambda b, i: (2, b, 0, 0))],
        out_specs=(pl.BlockSpec((1, tb, w), lambda b, i: (b, i, 0)),
                   pl.BlockSpec((1, tb, w), lambda b, i: (b, i, 0))),
        compiler_params=_cparams(2),
        name="dilated_attn",
    )(qkv, qkv, qkv)


def _dil_combine_kernel(o0_ref, l0_ref, o1_ref, l1_ref, o2_ref, l2_ref, oa_ref, nat_sc):
    tm = oa_ref.shape[0]
    n_half = DIL_GROUP_WIDTH // LANES

    def natural(ref, dil):
        if dil == 1:
            return ref[0, 0].astype(F32)
        rows = tm // dil
        for r in range(dil):
            blk = ref[0, r].astype(F32)
            for hf in range(n_half):
                nat_sc[hf, pl.ds(r, rows, stride=dil), :] = blk[:, hf * LANES:(hf + 1) * LANES]
        return jnp.concatenate([nat_sc[hf] for hf in range(n_half)], axis=1)

    dils = [d for _, d in DIL_GROUPS]
    lses = [natural(r, d) for r, d in zip((l0_ref, l1_ref, l2_ref), dils)]
    m = jnp.maximum(jnp.maximum(lses[0], lses[1]), lses[2])
    ws = [jnp.exp(l - m) for l in lses]
    den = ws[0] + ws[1] + ws[2]
    num = jnp.zeros_like(den)
    for w, r, d in zip(ws, (o0_ref, o1_ref, o2_ref), dils):
        num = num + w * natural(r, d)
    oa_ref[...] = (num / den).astype(BF16)


def _dil_combine(outs, lses, bsz, seq):
    tm = min(512, seq)
    spt = seq // tm
    in_specs, args = [], []
    for (o, l), (_, dil) in zip(zip(outs, lses), DIL_GROUPS):
        shape4 = (bsz, dil, seq // dil, DIL_GROUP_WIDTH)
        spec = pl.BlockSpec((1, dil, tm // dil, DIL_GROUP_WIDTH), lambda i: (i // spt, 0, i % spt, 0))
        in_specs += [spec, spec]
        args += [o.reshape(shape4), l.reshape(shape4)]
    return pl.pallas_call(
        _dil_combine_kernel,
        out_shape=jax.ShapeDtypeStruct((bsz * seq, DIL_GROUP_WIDTH), BF16),
        grid=(bsz * spt,),
        in_specs=in_specs,
        out_specs=pl.BlockSpec((tm, DIL_GROUP_WIDTH), lambda i: (i, 0)),
        scratch_shapes=[pltpu.VMEM((DIL_GROUP_WIDTH // LANES, tm, LANES), F32)],
        compiler_params=_cparams(1),
        name="dilated_combine",
    )(*args)


def _diff_attn_kernel(bounded_ref, q_ref, k_ref, vt_ref, lq1_ref, lk1_ref, lq2_ref, lk2_ref, sg_ref, o_ref,
                      l_sc, acc_sc, *, lam_init):
    tq = q_ref.shape[0]
    n_chunks = vt_ref.shape[0]
    q = q_ref[...]
    lane = lax.broadcasted_iota(jnp.int32, q.shape, 1)
    zero = jnp.zeros_like(q)
    qz = jnp.concatenate([jnp.where(lane < HEAD_DIM, q, zero), jnp.where(lane >= HEAD_DIM, q, zero)], axis=0)

    def scores(c):
        ks = pl.multiple_of(c * DIFF_TK, DIFF_TK)
        k = k_ref[pl.ds(ks, DIFF_TK), :]
        return lax.dot_general(k, qz, _NT, preferred_element_type=F32)

    @pl.when(bounded_ref[0] != 0)
    def _():
        l = jnp.zeros((1, 2 * tq), F32)
        acc = jnp.zeros((vt_ref.shape[1], 2 * tq), F32)
        for c in range(n_chunks):
            p = jnp.exp2(scores(c))
            l = l + jnp.sum(p, axis=0, keepdims=True)
            acc = acc + jnp.dot(vt_ref[c], p.astype(BF16), preferred_element_type=F32)
        l_sc[...] = l
        acc_sc[...] = acc

    @pl.when(bounded_ref[0] == 0)
    def _():
        def body(c, carry):
            m, l, acc = carry
            st = scores(c)
            m_new = jnp.maximum(m, jnp.max(st, axis=0, keepdims=True))
            alpha = jnp.exp2(m - m_new)
            p = jnp.exp2(st - m_new)
            l = alpha * l + jnp.sum(p, axis=0, keepdims=True)
            acc = alpha * acc + jnp.dot(vt_ref[c], p.astype(BF16), preferred_element_type=F32)
            return m_new, l, acc

        init = (jnp.full((1, 2 * tq), -jnp.inf, F32), jnp.zeros((1, 2 * tq), F32),
                jnp.zeros((vt_ref.shape[1], 2 * tq), F32))
        _, l, acc = lax.fori_loop(0, n_chunks, body, init)
        l_sc[...] = l
        acc_sc[...] = acc

    o = acc_sc[...] / l_sc[...]
    lam = (jnp.exp(jnp.sum(lq1_ref[...] * lk1_ref[...], axis=-1, keepdims=True))
           - jnp.exp(jnp.sum(lq2_ref[...] * lk2_ref[...], axis=-1, keepdims=True)) + lam_init)
    obt = o[:, :tq] - lam * o[:, tq:]
    ms = jnp.mean(obt * obt, axis=0, keepdims=True)
    obn = obt * lax.rsqrt(ms + EPS)
    o_ref[...] = (obn.T * (sg_ref[...] * (1.0 - lam_init))).astype(BF16)


def _diff_attn(qk, vt, score_bounded, lam_vecs, subln_g, bsz, seq, lam_init):
    t = qk.shape[0]
    tq = min(512, seq)
    qpt = seq // tq
    cps = seq // DIFF_TK
    hw = 2 * HEAD_DIM
    kern = functools.partial(_diff_attn_kernel, lam_init=lam_init)
    vec_spec = pl.BlockSpec((1, HEAD_DIM), lambda b, h, i: (0, 0))
    return pl.pallas_call(
        kern,
        out_shape=jax.ShapeDtypeStruct((t, DIFF_HEADS * hw), BF16),
        grid=(bsz, DIFF_HEADS, qpt),
        in_specs=[pl.BlockSpec(memory_space=pltpu.SMEM),
                  pl.BlockSpec((tq, hw), lambda b, h, i: (b * qpt + i, h)),
                  pl.BlockSpec((seq, hw), lambda b, h, i: (b, DIFF_HEADS + h)),
                  pl.BlockSpec((cps, hw, DIFF_TK), lambda b, h, i: (b, h, 0)),
                  vec_spec, vec_spec, vec_spec, vec_spec,
                  pl.BlockSpec((1, hw), lambda b, h, i: (0, 0))],
        out_specs=pl.BlockSpec((tq, hw), lambda b, h, i: (b * qpt + i, h)),
        scratch_shapes=[pltpu.VMEM((1, 2 * tq), F32), pltpu.VMEM((hw, 2 * tq), F32)],
        compiler_params=_cparams(3),
        name="diff_attn",
    )(score_bounded, qk, qk, vt, *lam_vecs, subln_g)


def _merge_kernel(oa_ref, ob_ref, ga_ref, gb_ref, x_ref, mod_ref, ng_ref, wpa_ref, wpb_ref, wo_ref,
                  x1_ref, h2_ref):
    ba = jnp.dot(oa_ref[...], wpa_ref[...], preferred_element_type=F32)
    bb = jnp.dot(ob_ref[...], wpb_ref[...], preferred_element_type=F32)
    mixed = ga_ref[...].astype(F32) * ba + gb_ref[...].astype(F32) * bb
    mo = jnp.dot(mixed.astype(BF16), wo_ref[...], preferred_element_type=F32)
    m = mod_ref[0]
    x1 = x_ref[...] + m[2:3] * mo
    x1_ref[...] = x1
    h2_ref[...] = _modulated_norm(x1, ng_ref[...], m[4:5], m[3:4]).astype(BF16)


def _merge(oa, ob, gates, x2, mod3, norm2_g, wpa, wpb, wo, seq):
    t, d = x2.shape
    tm = min(512, seq)
    spt = seq // tm
    full = lambda i: (0, 0)
    return pl.pallas_call(
        _merge_kernel,
        out_shape=(jax.ShapeDtypeStruct((t, d), F32), jax.ShapeDtypeStruct((t, d), BF16)),
        grid=(t // tm,),
        in_specs=[pl.BlockSpec((tm, oa.shape[1]), lambda i: (i, 0)),
                  pl.BlockSpec((tm, d), lambda i: (i, 0)),
                  pl.BlockSpec((tm, d), lambda i: (i, 0)),
                  pl.BlockSpec((tm, d), lambda i: (i, 1)),
                  pl.BlockSpec((tm, d), lambda i: (i, 0)),
                  pl.BlockSpec((1, 6, d), lambda i: (i // spt, 0, 0)),
                  pl.BlockSpec((1, d), full),
                  pl.BlockSpec(wpa.shape, full),
                  pl.BlockSpec(wpb.shape, full),
                  pl.BlockSpec(wo.shape, full)],
        out_specs=(pl.BlockSpec((tm, d), lambda i: (i, 0)), pl.BlockSpec((tm, d), lambda i: (i, 0))),
        compiler_params=_cparams(1),
        name="merge_outproj_norm2",
    )(oa, ob, gates, gates, x2, mod3, norm2_g, wpa, wpb, wo)


def _extract_topk(s, idx_f, val_sc, idx_sc):
    for k in range(PEER_TOPK):
        m = jnp.max(s, axis=0, keepdims=True)
        idx = jnp.min(jnp.where(s == m, idx_f, 1e9), axis=0, keepdims=True)
        val_sc[k:k + 1, :] = m
        idx_sc[k:k + 1, :] = idx
        s = jnp.where(idx_f == idx, -jnp.inf, s)


def _peer_topk_kernel(h_ref, wq_ref, sk_ref, e1_ref, e2_ref, g_ref,
                      q_sc, v1_sc, i1_sc, v2_sc, i2_sc, top_sc, lab_sc, e1t_sc, e2t_sc, gt_sc):
    tm = h_ref.shape[0]
    q = jnp.dot(h_ref[...], wq_ref[...], preferred_element_type=F32).astype(BF16)
    for piece in range(2 * PEER_HEADS):
        q_sc[piece] = q[:, piece * PEER_HALF:(piece + 1) * PEER_HALF]
    key_idx = lax.broadcasted_iota(jnp.int32, (PEER_N_KEYS, tm), 0).astype(F32)
    sub8 = lax.broadcasted_iota(jnp.int32, (8, tm), 0).astype(F32)
    neg = jnp.full((8, tm), -jnp.inf, F32)

    def head(h, carry):
        for side, (v_sc, i_sc) in enumerate(((v1_sc, i1_sc), (v2_sc, i2_sc))):
            qs = q_sc[2 * h + side]
            st = lax.dot_general(sk_ref[2 * h + side], qs, _NT, preferred_element_type=F32)
            _extract_topk(st, key_idx, v_sc, i_sc)
        v2lo = v2_sc[0:8, :]
        cands = [v1_sc[0:1, :] + v2lo, v1_sc[0:1, :] + v2_sc[8:16, :]]
        labels = [sub8, sub8 + 8.0]
        for a in range(1, 8):
            nb = PEER_TOPK // (a + 1)
            cands.append(jnp.where(sub8 < nb, v1_sc[a:a + 1, :] + v2lo, neg))
            labels.append(sub8 + float(a * PEER_TOPK))
        cands.append(v1_sc[8:16, :] + v2_sc[0:1, :])
        labels.append((sub8 + 8.0) * float(PEER_TOPK))
        cand = jnp.concatenate(cands, axis=0)
        label = jnp.concatenate(labels, axis=0)
        _extract_topk(cand, label, top_sc, lab_sc)
        top = top_sc[...]
        lab = lab_sc[...]
        a_sel = jnp.floor(lab * (1.0 / PEER_TOPK))
        b_sel = lab - a_sel * PEER_TOPK
        e1 = jnp.zeros_like(top)
        e2 = jnp.zeros_like(top)
        for r in range(PEER_TOPK):
            e1 = jnp.where(a_sel == float(r), i1_sc[r:r + 1, :], e1)
            e2 = jnp.where(b_sel == float(r), i2_sc[r:r + 1, :], e2)
        p = jnp.exp(top - jnp.max(top, axis=0, keepdims=True))
        gate = p / jnp.sum(p, axis=0, keepdims=True)
        row = pl.multiple_of(h * PEER_TOPK, PEER_TOPK)
        e1t_sc[pl.ds(row, PEER_TOPK), :] = e1
        e2t_sc[pl.ds(row, PEER_TOPK), :] = e2
        gt_sc[pl.ds(row, PEER_TOPK), :] = gate
        return carry

    lax.fori_loop(0, PEER_HEADS, head, 0)
    e1_ref[...] = e1t_sc[...].T
    e2_ref[...] = e2t_sc[...].T
    g_ref[...] = gt_sc[...].T


def _peer_topk(h2, wq, sk):
    t, d = h2.shape
    tm = min(512, t)
    nsel = PEER_HEADS * PEER_TOPK
    out = jax.ShapeDtypeStruct((t, nsel), F32)
    spec = pl.BlockSpec((tm, nsel), lambda i: (i, 0))
    small = lambda: pltpu.VMEM((PEER_TOPK, tm), F32)
    big = lambda: pltpu.VMEM((nsel, tm), F32)
    return pl.pallas_call(
        _peer_topk_kernel,
        out_shape=(out, out, out),
        grid=(t // tm,),
        in_specs=[pl.BlockSpec((tm, d), lambda i: (i, 0)),
                  pl.BlockSpec(wq.shape, lambda i: (0, 0)),
                  pl.BlockSpec(sk.shape, lambda i: (0, 0, 0))],
        out_specs=(spec, spec, spec),
        scratch_shapes=[pltpu.VMEM((2 * PEER_HEADS, tm, PEER_HALF), BF16),
                        small(), small(), small(), small(), small(), small(), big(), big(), big()],
        compiler_params=_cparams(1),
        name="peer_topk",
    )(h2, wq, sk)


def _peer_gates_kernel(e1_ref, e2_ref, g_ref, o_ref, g_sc, *, pitch):
    tm = e1_ref.shape[0]
    row_idx = lax.broadcasted_iota(jnp.int32, (PEER_N_KEYS, e1_ref.shape[1]), 0).astype(F32)

    def token(t, carry):
        r1 = e1_ref[pl.ds(t, 1), :]
        r2 = e2_ref[pl.ds(t, 1), :]
        gt = g_ref[pl.ds(t, 1), :]
        sel_i = jnp.where(row_idx == r1, 1.0, 0.0).astype(BF16)
        sel_j = jnp.where(row_idx == r2, gt, 0.0).astype(BF16)
        gm = lax.dot_general(sel_i, sel_j, _NT, preferred_element_type=F32)
        g_sc[pl.ds(t, PEER_N_KEYS, stride=pitch), :] = gm
        return carry

    lax.fori_loop(0, tm, token, 0, unroll=16)

    def emit(i, carry):
        start = pl.multiple_of(i * pitch, 8)
        o_ref[i] = g_sc[pl.ds(start, tm), :].astype(BF16)
        return carry

    lax.fori_loop(0, PEER_N_KEYS, emit, 0)


def _peer_gates(e1, e2, gate):
    t, nsel = e1.shape
    tm = min(128, t)
    pitch = tm + 8
    spec = pl.BlockSpec((tm, nsel), lambda i: (i, 0))
    return pl.pallas_call(
        functools.partial(_peer_gates_kernel, pitch=pitch),
        out_shape=jax.ShapeDtypeStruct((PEER_N_KEYS, t, PEER_N_KEYS), BF16),
        grid=(t // tm,),
        in_specs=[spec, spec, spec],
        out_specs=pl.BlockSpec((PEER_N_KEYS, tm, PEER_N_KEYS), lambda i: (0, i, 0)),
        scratch_shapes=[pltpu.VMEM((PEER_N_KEYS * pitch, PEER_N_KEYS), F32)],
        compiler_params=_cparams(1),
        name="peer_gate_matrix",
    )(e1, e2, gate)


def _peer_dense_kernel(h_ref, u_ref, v_ref, g_ref, x1_ref, mod_ref, o_ref, acc_sc):
    c = pl.program_id(1)

    @pl.when(c == 0)
    def _():
        acc_sc[...] = jnp.zeros_like(acc_sc)

    a = lax.dot_general(h_ref[...], u_ref[...], _NT, preferred_element_type=F32)
    act = 0.5 * a * (1.0 + lax.erf(a * (1.0 / math.sqrt(2.0))))
    ws = [(act[:, ib * LANES:(ib + 1) * LANES] * g_ref[ib].astype(F32)).astype(BF16)
          for ib in range(g_ref.shape[0])]
    w = jnp.concatenate(ws, axis=1)
    acc_sc[...] += jnp.dot(w, v_ref[...], preferred_element_type=F32)

    @pl.when(c == pl.num_programs(1) - 1)
    def _():
        o_ref[...] = x1_ref[...] + mod_ref[0][5:6] * acc_sc[...]


def _peer_dense(h2, u, v, gmat, x1, mod3, seq):
    t, d = h2.shape
    n_exp = u.shape[0]
    tm = min(1024, seq)
    spt = seq // tm
    tn = 1024
    return pl.pallas_call(
        _peer_dense_kernel,
        out_shape=jax.ShapeDtypeStruct((t, d), F32),
        grid=(t // tm, n_exp // tn),
        in_specs=[pl.BlockSpec((tm, d), lambda i, c: (i, 0)),
                  pl.BlockSpec((tn, d), lambda i, c: (c, 0)),
                  pl.BlockSpec((tn, d), lambda i, c: (c, 0)),
                  pl.BlockSpec((tn // PEER_N_KEYS, tm, PEER_N_KEYS), lambda i, c: (c, i, 0)),
                  pl.BlockSpec((tm, d), lambda i, c: (i, 0)),
                  pl.BlockSpec((1, 6, d), lambda i, c: (i // spt, 0, 0))],
        out_specs=pl.BlockSpec((tm, d), lambda i, c: (i, 0)),
        scratch_shapes=[pltpu.VMEM((tm, d), F32)],
        compiler_params=_cparams(2),
        name="peer_dense",
    )(h2, u, v, gmat, x1, mod3)


def _layer(x, c, cos, sin, lam_init, w_ada, b_ada, norm1_g, w_in, b_gate, qn_a, kn_a, w_proj_a,
           qn_b, kn_b, lam_q1, lam_k1, lam_q2, lam_k2, subln_g, w_proj_b, w_out, norm2_g,
           w_query, sub_keys, expert_u, expert_v):
    bsz, seq, d = x.shape
    t = bsz * seq
    x2 = x.reshape(t, d)
    mod3 = _ada(c, w_ada, b_ada).reshape(bsz, 6, d)
    gmat = _group_sum_matrix()
    scale = HEAD_DIM ** -0.5

    dil_w = len(DIL_GROUPS) * DIL_GROUP_WIDTH
    diff_w = DIFF_HEADS * 2 * HEAD_DIM
    a_cols = 3 * dil_w
    w_in_bf = w_in.astype(BF16)
    ones = lambda n: jnp.ones((n,), F32)
    gain_a = jnp.concatenate([jnp.tile(qn_a, dil_w // HEAD_DIM) * scale, jnp.tile(kn_a, dil_w // HEAD_DIM),
                              ones(dil_w)]).reshape(1, a_cols)
    qkv_groups = _inproj_a(x2, mod3, norm1_g.reshape(1, d), w_in_bf[:, :a_cols], gain_a, cos, sin, gmat, bsz, seq)

    n_b = w_in.shape[1] - a_cols
    gain_b = jnp.concatenate([jnp.tile(qn_b, diff_w // HEAD_DIM) * (scale * LOG2E),
                              jnp.tile(kn_b, diff_w // HEAD_DIM), ones(n_b - 2 * diff_w)]).reshape(1, n_b)
    bias_b = jnp.concatenate([jnp.zeros((3 * diff_w,), F32), b_gate]).reshape(1, n_b)
    qk_b, vt_b, gates = _inproj_b(x2, mod3, norm1_g.reshape(1, d), w_in_bf[:, a_cols:], gain_b, bias_b,
                                  cos, sin, gmat, seq, diff_w, diff_w)

    outs, lses = [], []
    for qkv in qkv_groups:
        o, l = _dilated_attn(qkv)
        outs.append(o)
        lses.append(l)
    oa = _dil_combine(outs, lses, bsz, seq)

    lam_vecs = [v.reshape(1, HEAD_DIM) for v in (lam_q1, lam_k1, lam_q2, lam_k2)]
    score_cap = (HEAD_DIM * scale * LOG2E * 1.01) * jnp.max(jnp.abs(qn_b)) * jnp.max(jnp.abs(kn_b))
    score_bounded = (score_cap <= DIFF_SCORE_BOUND).astype(jnp.int32).reshape(1)
    ob = _diff_attn(qk_b, vt_b, score_bounded, lam_vecs, subln_g.reshape(1, 2 * HEAD_DIM), bsz, seq, lam_init)

    x1, h2 = _merge(oa, ob, gates, x2, mod3, norm2_g.reshape(1, d), w_proj_a.astype(BF16),
                    w_proj_b.astype(BF16), w_out.astype(BF16), seq)

    sk = sub_keys.astype(BF16).reshape(PEER_HEADS * 2, PEER_N_KEYS, PEER_HALF)
    e1, e2, gate = _peer_topk(h2, w_query.astype(BF16), sk)
    gdense = _peer_gates(e1, e2, gate)
    out = _peer_dense(h2, expert_u.astype(BF16), expert_v.astype(BF16), gdense, x1, mod3, seq)
    return out.reshape(bsz, seq, d)


def kernel(x, c, positions, w_ada, b_ada, norm1_g, w_in, b_gate, qn_a, kn_a, w_proj_a, qn_b, kn_b,
           lam_q1, lam_k1, lam_q2, lam_k2, subln_g, w_proj_b, w_out, norm2_g, w_query, sub_keys,
           expert_u, expert_v):
    cos, sin = _rope_tables(positions)
    for l in range(w_ada.shape[0]):
        lam_init = 0.8 - 0.6 * math.exp(-0.3 * l)
        x = _layer(x, c, cos, sin, lam_init, w_ada[l], b_ada[l], norm1_g[l], w_in[l], b_gate[l],
                   qn_a[l], kn_a[l], w_proj_a[l], qn_b[l], kn_b[l], lam_q1[l], lam_k1[l], lam_q2[l],
                   lam_k2[l], subln_g[l], w_proj_b[l], w_out[l], norm2_g[l], w_query[l], sub_keys[l],
                   expert_u[l], expert_v[l])
    return x
```

```python
import functools
import math

import jax
import jax.numpy as jnp
from jax import lax
from jax.experimental import pallas as pl
from jax.experimental.pallas import tpu as pltpu

F32 = jnp.float32
BF16 = jnp.bfloat16

HEAD_DIM = 64
ROPE_THETA = 10000.0
EPS = 1e-6
NEG_INF = -1e30

DIL_GROUPS = ((128, 1), (512, 4), (2048, 16))
DIL_HEADS_PER_GROUP = 4
DIL_GROUP_WIDTH = DIL_HEADS_PER_GROUP * HEAD_DIM
DIL_SIDE = 64

DIFF_HEADS = 8
DIFF_TK = 512
LOG2E = 1.4426950408889634
DIFF_SCORE_BOUND = 60.0
PEER_HEADS = 8
PEER_N_KEYS = 128
PEER_TOPK = 16
PEER_HALF = 128

LANES = 128
VMEM_LIMIT = 56 * 1024 * 1024

_NT = (((1,), (1,)), ((), ()))


def _cparams(n_axes):
    return pltpu.CompilerParams(dimension_semantics=("arbitrary",) * n_axes,
                                vmem_limit_bytes=VMEM_LIMIT)


def _ada_kernel(c_ref, w_ref, b_ref, o_ref):
    c = c_ref[...]
    sc = (c * jax.nn.sigmoid(c)).astype(BF16)
    o_ref[...] = jnp.dot(sc, w_ref[...].astype(BF16), preferred_element_type=F32) + b_ref[...]


def _ada(c, w_ada, b_ada):
    bsz, d = c.shape
    n = w_ada.shape[1]
    tn = 1024
    return pl.pallas_call(
        _ada_kernel,
        out_shape=jax.ShapeDtypeStruct((bsz, n), F32),
        grid=(n // tn,),
        in_specs=[pl.BlockSpec((bsz, d), lambda j: (0, 0)),
                  pl.BlockSpec((d, tn), lambda j: (0, j)),
                  pl.BlockSpec((1, tn), lambda j: (0, j))],
        out_specs=pl.BlockSpec((bsz, tn), lambda j: (0, j)),
        compiler_params=_cparams(1),
        name="ada_mod",
    )(c, w_ada, b_ada.reshape(1, n))


def _rope_kernel(pos_ref, freq_ref, cos_ref, sin_ref):
    ang = pos_ref[...] * freq_ref[...]
    lane = lax.broadcasted_iota(jnp.int32, ang.shape, 1)
    first_half = (lane % HEAD_DIM) < (HEAD_DIM // 2)
    cos_ref[...] = jnp.cos(ang)
    s = jnp.sin(ang)
    sin_ref[...] = jnp.where(first_half, -s, s)


def _rope_tables(positions):
    t = positions.size
    tm = min(1024, t)
    pos = positions.reshape(t, 1).astype(F32)
    inv_freq = 1.0 / (ROPE_THETA ** (jnp.arange(0, HEAD_DIM, 2, dtype=F32) / HEAD_DIM))
    freq = jnp.tile(inv_freq, LANES // (HEAD_DIM // 2)).reshape(1, LANES)
    return pl.pallas_call(
        _rope_kernel,
        out_shape=(jax.ShapeDtypeStruct((t, LANES), F32), jax.ShapeDtypeStruct((t, LANES), F32)),
        grid=(t // tm,),
        in_specs=[pl.BlockSpec((tm, 1), lambda i: (i, 0)),
                  pl.BlockSpec((1, LANES), lambda i: (0, 0))],
        out_specs=(pl.BlockSpec((tm, LANES), lambda i: (i, 0)),
                   pl.BlockSpec((tm, LANES), lambda i: (i, 0))),
        compiler_params=_cparams(1),
        name="rope_tables",
    )(pos, freq)


def _modulated_norm(x, g, scale, shift):
    ms = jnp.mean(x * x, axis=-1, keepdims=True)
    return (x * lax.rsqrt(ms + EPS)) * g * (1.0 + scale) + shift


def _head_norm_rope(acc, gain, cos, sin, gmat):
    ss = jnp.dot((acc * acc).astype(BF16), gmat, preferred_element_type=F32)
    yn = acc * lax.rsqrt(ss * (1.0 / HEAD_DIM) + EPS) * gain
    lane = lax.broadcasted_iota(jnp.int32, cos.shape, 1)
    first_half = (lane % HEAD_DIM) < (HEAD_DIM // 2)
    outs = []
    for hf in range(acc.shape[1] // LANES):
        y = yn[:, hf * LANES:(hf + 1) * LANES]
        up = pltpu.roll(y, HEAD_DIM // 2, axis=1)
        down = pltpu.roll(y, LANES - HEAD_DIM // 2, axis=1)
        swapped = jnp.where(first_half, down, up)
        outs.append(y * cos + swapped * sin)
    return jnp.concatenate(outs, axis=1)


def _group_sum_matrix():
    r = lax.broadcasted_iota(jnp.int32, (2 * LANES, 2 * LANES), 0) // HEAD_DIM
    c = lax.broadcasted_iota(jnp.int32, (2 * LANES, 2 * LANES), 1) // HEAD_DIM
    return (r == c).astype(BF16)


def _inproj_a_kernel(x_ref, mod_ref, ng_ref, w_ref, gain_ref, cos_ref, sin_ref, gmat_ref,
                     o0_ref, o1_ref, o2_ref, h_sc, y_sc):
    j = pl.program_id(1)
    tm = x_ref.shape[0]

    @pl.when(j == 0)
    def _():
        m = mod_ref[0]
        h_sc[...] = _modulated_norm(x_ref[...], ng_ref[...], m[1:2], m[0:1]).astype(BF16)

    acc = jnp.dot(h_sc[...], w_ref[...], preferred_element_type=F32)
    outs = (o0_ref, o1_ref, o2_ref)

    def emit(get_piece):
        for g, (_, dil) in enumerate(DIL_GROUPS):
            y = get_piece(g)
            if dil == 1:
                outs[g][0, 0, 0] = y.astype(BF16)
            else:
                rows = tm // dil
                for hf in range(DIL_GROUP_WIDTH // LANES):
                    y_sc[hf] = y[:, hf * LANES:(hf + 1) * LANES]
                for r in range(dil):
                    piece = jnp.concatenate(
                        [y_sc[hf, pl.ds(r, rows, stride=dil), :] for hf in range(DIL_GROUP_WIDTH // LANES)],
                        axis=1)
                    outs[g][0, 0, r] = piece.astype(BF16)

    @pl.when(j < 2)
    def _():
        cos = cos_ref[...]
        sin = sin_ref[...]
        gmat = gmat_ref[...]
        emit(lambda g: _head_norm_rope(acc[:, g * DIL_GROUP_WIDTH:(g + 1) * DIL_GROUP_WIDTH],
                                       gain_ref[:, g * DIL_GROUP_WIDTH:(g + 1) * DIL_GROUP_WIDTH],
                                       cos, sin, gmat))

    @pl.when(j == 2)
    def _():
        emit(lambda g: acc[:, g * DIL_GROUP_WIDTH:(g + 1) * DIL_GROUP_WIDTH])


def _inproj_a(x2, mod3, norm_g, w_a, gain_a, cos, sin, gmat, bsz, seq):
    t, d = x2.shape
    tm = min(1024, seq)
    spt = seq // tm
    n = w_a.shape[1]
    tn = n // 3
    out_shapes, out_specs = [], []
    for _, dil in DIL_GROUPS:
        out_shapes.append(jax.ShapeDtypeStruct((3, bsz, dil, seq // dil, DIL_GROUP_WIDTH), BF16))
        out_specs.append(pl.BlockSpec((1, 1, dil, tm // dil, DIL_GROUP_WIDTH),
                                      lambda i, j: (j, i // spt, 0, i % spt, 0)))
    return pl.pallas_call(
        _inproj_a_kernel,
        out_shape=tuple(out_shapes),
        grid=(t // tm, 3),
        in_specs=[pl.BlockSpec((tm, d), lambda i, j: (i, 0)),
                  pl.BlockSpec((1, 6, d), lambda i, j: (i // spt, 0, 0)),
                  pl.BlockSpec((1, d), lambda i, j: (0, 0)),
                  pl.BlockSpec((d, tn), lambda i, j: (0, j)),
                  pl.BlockSpec((1, tn), lambda i, j: (0, j)),
                  pl.BlockSpec((tm, LANES), lambda i, j: (i, 0)),
                  pl.BlockSpec((tm, LANES), lambda i, j: (i, 0)),
                  pl.BlockSpec((2 * LANES, 2 * LANES), lambda i, j: (0, 0))],
        out_specs=tuple(out_specs),
        scratch_shapes=[pltpu.VMEM((tm, d), BF16),
                        pltpu.VMEM((DIL_GROUP_WIDTH // LANES, tm, LANES), F32)],
        compiler_params=_cparams(2),
        name="inproj_dilated",
    )(x2, mod3, norm_g, w_a, gain_a, cos, sin, gmat)


def _inproj_b_kernel(x_ref, mod_ref, ng_ref, w_ref, gain_ref, bias_ref, cos_ref, sin_ref, gmat_ref,
                     qk_ref, vt_ref, gate_ref, h_sc, *, n_rope_tiles, n_plain_tiles):
    j = pl.program_id(1)

    @pl.when(j == 0)
    def _():
        m = mod_ref[0]
        h_sc[...] = _modulated_norm(x_ref[...], ng_ref[...], m[1:2], m[0:1]).astype(BF16)

    acc = jnp.dot(h_sc[...], w_ref[...], preferred_element_type=F32)
    tn = acc.shape[1]

    @pl.when(j < n_rope_tiles)
    def _():
        cos = cos_ref[...]
        sin = sin_ref[...]
        gmat = gmat_ref[...]
        for p in range(tn // (2 * LANES)):
            sl = slice(p * 2 * LANES, (p + 1) * 2 * LANES)
            qk_ref[:, sl] = _head_norm_rope(acc[:, sl], gain_ref[:, sl], cos, sin, gmat).astype(BF16)

    @pl.when((j >= n_rope_tiles) & (j < n_rope_tiles + n_plain_tiles))
    def _():
        for cc in range(vt_ref.shape[0]):
            vt_ref[cc] = acc[cc * DIFF_TK:(cc + 1) * DIFF_TK, :].T.astype(BF16)

    @pl.when(j >= n_rope_tiles + n_plain_tiles)
    def _():
        gate_ref[...] = jax.nn.sigmoid(acc + bias_ref[...]).astype(BF16)


def _inproj_b(x2, mod3, norm_g, w_b, gain_b, bias_b, cos, sin, gmat, seq, qk_width, v_width):
    t, d = x2.shape
    tm = min(1024, seq)
    spt = seq // tm
    n = w_b.shape[1]
    tn = 512
    n_rope, n_plain = 2 * qk_width // tn, v_width // tn
    n_gate = n // tn - n_rope - n_plain
    kern = functools.partial(_inproj_b_kernel, n_rope_tiles=n_rope, n_plain_tiles=n_plain)
    return pl.pallas_call(
        kern,
        out_shape=(jax.ShapeDtypeStruct((t, n_rope * tn), BF16),
                   jax.ShapeDtypeStruct((t // DIFF_TK, v_width, DIFF_TK), BF16),
                   jax.ShapeDtypeStruct((t, n_gate * tn), BF16)),
        grid=(t // tm, n // tn),
        in_specs=[pl.BlockSpec((tm, d), lambda i, j: (i, 0)),
                  pl.BlockSpec((1, 6, d), lambda i, j: (i // spt, 0, 0)),
                  pl.BlockSpec((1, d), lambda i, j: (0, 0)),
                  pl.BlockSpec((d, tn), lambda i, j: (0, j)),
                  pl.BlockSpec((1, tn), lambda i, j: (0, j)),
                  pl.BlockSpec((1, tn), lambda i, j: (0, j)),
                  pl.BlockSpec((tm, LANES), lambda i, j: (i, 0)),
                  pl.BlockSpec((tm, LANES), lambda i, j: (i, 0)),
                  pl.BlockSpec((2 * LANES, 2 * LANES), lambda i, j: (0, 0))],
        out_specs=(pl.BlockSpec((tm, tn), lambda i, j: (i, jnp.minimum(j, n_rope - 1))),
                   pl.BlockSpec((tm // DIFF_TK, tn, DIFF_TK),
                                lambda i, j: (i, jnp.clip(j - n_rope, 0, n_plain - 1), 0)),
                   pl.BlockSpec((tm, tn), lambda i, j: (i, jnp.clip(j - n_rope - n_plain, 0, n_gate - 1)))),
        scratch_shapes=[pltpu.VMEM((tm, d), BF16)],
        compiler_params=_cparams(2),
        name="inproj_diff_gates",
    )(x2, mod3, norm_g, w_b, gain_b, bias_b, cos, sin, gmat)


def _dilated_attn_kernel(q_ref, k_ref, v_ref, o_ref, lse_ref, *, tq, kw, length, n_sub):
    qi = pl.program_id(1)
    lane_head = lax.broadcasted_iota(jnp.int32, (tq, DIL_GROUP_WIDTH), 1) // HEAD_DIM
    for sub in range(n_sub):
        rows = slice(sub * tq, (sub + 1) * tq)
        q0 = (qi * n_sub + sub) * tq
        kstart = jnp.clip(q0 - DIL_SIDE, 0, length - kw)
        kstart = pl.multiple_of(kstart, DIL_SIDE)
        q = q_ref[0, 0, rows, :]
        k = k_ref[0, 0, pl.ds(kstart, kw), :]
        v = v_ref[0, 0, pl.ds(kstart, kw), :]
        n_h = DIL_HEADS_PER_GROUP
        qs = jnp.concatenate([jnp.where(lane_head == h, q, jnp.zeros_like(q)) for h in range(n_h)], axis=0)
        s = lax.dot_general(qs, k, _NT, preferred_element_type=F32)
        row = lax.broadcasted_iota(jnp.int32, (n_h * tq, kw), 0)
        qidx = q0 + row % tq
        kidx = kstart + lax.broadcasted_iota(jnp.int32, (n_h * tq, kw), 1)
        s = jnp.where(jnp.abs(kidx - qidx) <= DIL_SIDE, s, NEG_INF)
        m = jnp.max(s, axis=-1, keepdims=True)
        p = jnp.exp(s - m)
        l = jnp.sum(p, axis=-1, keepdims=True)
        pb = p.astype(BF16)
        lse = m + jnp.log(l)
        out = jnp.zeros((tq, DIL_GROUP_WIDTH), F32)
        lse_full = jnp.zeros((tq, DIL_GROUP_WIDTH), F32)
        for h in range(n_h):
            in_head = lane_head == h
            hr = slice(h * tq, (h + 1) * tq)
            oh = jnp.dot(pb[hr], v, preferred_element_type=F32)
            out = jnp.where(in_head, oh / l[hr], out)
            lse_full = jnp.where(in_head, lse[hr], lse_full)
        o_ref[0, rows, :] = out.astype(BF16)
        lse_ref[0, rows, :] = lse_full


def _dilated_attn(qkv):
    _, bsz, dil, length, w = qkv.shape
    bd = bsz * dil
    qkv = qkv.reshape(3, bd, length, w)
    tq = min(128, length)
    kw = min(tq + 2 * DIL_SIDE, length)
    n_sub = min(4, length // tq)
    tb = tq * n_sub
    kern = functools.partial(_dilated_attn_kernel, tq=tq, kw=kw, length=length, n_sub=n_sub)
    return pl.pallas_call(
        kern,
        out_shape=(jax.ShapeDtypeStruct((bd, length, w), BF16), jax.ShapeDtypeStruct((bd, length, w), F32)),
        grid=(bd, length // tb),
        in_specs=[pl.BlockSpec((1, 1, tb, w), lambda b, i: (0, b, i, 0)),
                  pl.BlockSpec((1, 1, length, w), lambda b, i: (1, b, 0, 0)),
                  pl.BlockSpec((1, 1, length, w), lambda b, i: (2, b, 0, 0))],
        out_specs=(pl.BlockSpec((1, tb, w), lambda b, i: (b, i, 0)),
                   pl.BlockSpec((1, tb, w), lambda b, i: (b, i, 0))),
        compiler_params=_cparams(2),
        name="dilated_attn",
    )(qkv, qkv, qkv)


def _dil_combine_kernel(o0_ref, l0_ref, o1_ref, l1_ref, o2_ref, l2_ref, oa_ref, nat_sc):
    tm = oa_ref.shape[0]
    n_half = DIL_GROUP_WIDTH // LANES

    def natural(ref, dil):
        if dil == 1:
            return ref[0, 0].astype(F32)
        rows = tm // dil
        for r in range(dil):
            blk = ref[0, r].astype(F32)
            for hf in range(n_half):
                nat_sc[hf, pl.ds(r, rows, stride=dil), :] = blk[:, hf * LANES:(hf + 1) * LANES]
        return jnp.concatenate([nat_sc[hf] for hf in range(n_half)], axis=1)

    dils = [d for _, d in DIL_GROUPS]
    lses = [natural(r, d) for r, d in zip((l0_ref, l1_ref, l2_ref), dils)]
    m = jnp.maximum(jnp.maximum(lses[0], lses[1]), lses[2])
    ws = [jnp.exp(l - m) for l in lses]
    den = ws[0] + ws[1] + ws[2]
    num = jnp.zeros_like(den)
    for w, r, d in zip(ws, (o0_ref, o1_ref, o2_ref), dils):
        num = num + w * natural(r, d)
    oa_ref[...] = (num / den).astype(BF16)


def _dil_combine(outs, lses, bsz, seq):
    tm = min(512, seq)
    spt = seq // tm
    in_specs, args = [], []
    for (o, l), (_, dil) in zip(zip(outs, lses), DIL_GROUPS):
        shape4 = (bsz, dil, seq // dil, DIL_GROUP_WIDTH)
        spec = pl.BlockSpec((1, dil, tm // dil, DIL_GROUP_WIDTH), lambda i: (i // spt, 0, i % spt, 0))
        in_specs += [spec, spec]
        args += [o.reshape(shape4), l.reshape(shape4)]
    return pl.pallas_call(
        _dil_combine_kernel,
        out_shape=jax.ShapeDtypeStruct((bsz * seq, DIL_GROUP_WIDTH), BF16),
        grid=(bsz * spt,),
        in_specs=in_specs,
        out_specs=pl.BlockSpec((tm, DIL_GROUP_WIDTH), lambda i: (i, 0)),
        scratch_shapes=[pltpu.VMEM((DIL_GROUP_WIDTH // LANES, tm, LANES), F32)],
        compiler_params=_cparams(1),
        name="dilated_combine",
    )(*args)


def _diff_attn_kernel(bounded_ref, q_ref, k_ref, vt_ref, lq1_ref, lk1_ref, lq2_ref, lk2_ref, sg_ref, o_ref,
                      l_sc, acc_sc, *, lam_init):
    tq = q_ref.shape[0]
    n_chunks = vt_ref.shape[0]
    q = q_ref[...]
    lane = lax.broadcasted_iota(jnp.int32, q.shape, 1)
    zero = jnp.zeros_like(q)
    qz = jnp.concatenate([jnp.where(lane < HEAD_DIM, q, zero), jnp.where(lane >= HEAD_DIM, q, zero)], axis=0)

    def scores(c):
        ks = pl.multiple_of(c * DIFF_TK, DIFF_TK)
        k = k_ref[pl.ds(ks, DIFF_TK), :]
        return lax.dot_general(k, qz, _NT, preferred_element_type=F32)

    @pl.when(bounded_ref[0] != 0)
    def _():
        l = jnp.zeros((1, 2 * tq), F32)
        acc = jnp.zeros((vt_ref.shape[1], 2 * tq), F32)
        for c in range(n_chunks):
            p = jnp.exp2(scores(c))
            l = l + jnp.sum(p, axis=0, keepdims=True)
            acc = acc + jnp.dot(vt_ref[c], p.astype(BF16), preferred_element_type=F32)
        l_sc[...] = l
        acc_sc[...] = acc

    @pl.when(bounded_ref[0] == 0)
    def _():
        def body(c, carry):
            m, l, acc = carry
            st = scores(c)
            m_new = jnp.maximum(m, jnp.max(st, axis=0, keepdims=True))
            alpha = jnp.exp2(m - m_new)
            p = jnp.exp2(st - m_new)
            l = alpha * l + jnp.sum(p, axis=0, keepdims=True)
            acc = alpha * acc + jnp.dot(vt_ref[c], p.astype(BF16), preferred_element_type=F32)
            return m_new, l, acc

        init = (jnp.full((1, 2 * tq), -jnp.inf, F32), jnp.zeros((1, 2 * tq), F32),
                jnp.zeros((vt_ref.shape[1], 2 * tq), F32))
        _, l, acc = lax.fori_loop(0, n_chunks, body, init)
        l_sc[...] = l
        acc_sc[...] = acc

    o = acc_sc[...] / l_sc[...]
    lam = (jnp.exp(jnp.sum(lq1_ref[...] * lk1_ref[...], axis=-1, keepdims=True))
           - jnp.exp(jnp.sum(lq2_ref[...] * lk2_ref[...], axis=-1, keepdims=True)) + lam_init)
    obt = o[:, :tq] - lam * o[:, tq:]
    ms = jnp.mean(obt * obt, axis=0, keepdims=True)
    obn = obt * lax.rsqrt(ms + EPS)
    o_ref[...] = (obn.T * (sg_ref[...] * (1.0 - lam_init))).astype(BF16)


def _diff_attn(qk, vt, score_bounded, lam_vecs, subln_g, bsz, seq, lam_init):
    t = qk.shape[0]
    tq = min(512, seq)
    qpt = seq // tq
    cps = seq // DIFF_TK
    hw = 2 * HEAD_DIM
    kern = functools.partial(_diff_attn_kernel, lam_init=lam_init)
    vec_spec = pl.BlockSpec((1, HEAD_DIM), lambda b, h, i: (0, 0))
    return pl.pallas_call(
        kern,
        out_shape=jax.ShapeDtypeStruct((t, DIFF_HEADS * hw), BF16),
        grid=(bsz, DIFF_HEADS, qpt),
        in_specs=[pl.BlockSpec(memory_space=pltpu.SMEM),
                  pl.BlockSpec((tq, hw), lambda b, h, i: (b * qpt + i, h)),
                  pl.BlockSpec((seq, hw), lambda b, h, i: (b, DIFF_HEADS + h)),
                  pl.BlockSpec((cps, hw, DIFF_TK), lambda b, h, i: (b, h, 0)),
                  vec_spec, vec_spec, vec_spec, vec_spec,
                  pl.BlockSpec((1, hw), lambda b, h, i: (0, 0))],
        out_specs=pl.BlockSpec((tq, hw), lambda b, h, i: (b * qpt + i, h)),
        scratch_shapes=[pltpu.VMEM((1, 2 * tq), F32), pltpu.VMEM((hw, 2 * tq), F32)],
        compiler_params=_cparams(3),
        name="diff_attn",
    )(score_bounded, qk, qk, vt, *lam_vecs, subln_g)


def _merge_kernel(oa_ref, ob_ref, ga_ref, gb_ref, x_ref, mod_ref, ng_ref, wpa_ref, wpb_ref, wo_ref,
                  x1_ref, h2_ref):
    ba = jnp.dot(oa_ref[...], wpa_ref[...], preferred_element_type=F32)
    bb = jnp.dot(ob_ref[...], wpb_ref[...], preferred_element_type=F32)
    mixed = ga_ref[...].astype(F32) * ba + gb_ref[...].astype(F32) * bb
    mo = jnp.dot(mixed.astype(BF16), wo_ref[...], preferred_element_type=F32)
    m = mod_ref[0]
    x1 = x_ref[...] + m[2:3] * mo
    x1_ref[...] = x1
    h2_ref[...] = _modulated_norm(x1, ng_ref[...], m[4:5], m[3:4]).astype(BF16)


def _merge(oa, ob, gates, x2, mod3, norm2_g, wpa, wpb, wo, seq):
    t, d = x2.shape
    tm = min(512, seq)
    spt = seq // tm
    full = lambda i: (0, 0)
    return pl.pallas_call(
        _merge_kernel,
        out_shape=(jax.ShapeDtypeStruct((t, d), F32), jax.ShapeDtypeStruct((t, d), BF16)),
        grid=(t // tm,),
        in_specs=[pl.BlockSpec((tm, oa.shape[1]), lambda i: (i, 0)),
                  pl.BlockSpec((tm, d), lambda i: (i, 0)),
                  pl.BlockSpec((tm, d), lambda i: (i, 0)),
                  pl.BlockSpec((tm, d), lambda i: (i, 1)),
                  pl.BlockSpec((tm, d), lambda i: (i, 0)),
                  pl.BlockSpec((1, 6, d), lambda i: (i // spt, 0, 0)),
                  pl.BlockSpec((1, d), full),
                  pl.BlockSpec(wpa.shape, full),
                  pl.BlockSpec(wpb.shape, full),
                  pl.BlockSpec(wo.shape, full)],
        out_specs=(pl.BlockSpec((tm, d), lambda i: (i, 0)), pl.BlockSpec((tm, d), lambda i: (i, 0))),
        compiler_params=_cparams(1),
        name="merge_outproj_norm2",
    )(oa, ob, gates, gates, x2, mod3, norm2_g, wpa, wpb, wo)


def _extract_topk(s, idx_f, val_sc, idx_sc):
    for k in range(PEER_TOPK):
        m = jnp.max(s, axis=0, keepdims=True)
        idx = jnp.min(jnp.where(s == m, idx_f, 1e9), axis=0, keepdims=True)
        val_sc[k:k + 1, :] = m
        idx_sc[k:k + 1, :] = idx
        s = jnp.where(idx_f == idx, -jnp.inf, s)


def _peer_topk_kernel(h_ref, wq_ref, sk_ref, e1_ref, e2_ref, g_ref,
                      q_sc, v1_sc, i1_sc, v2_sc, i2_sc, top_sc, lab_sc, e1t_sc, e2t_sc, gt_sc):
    tm = h_ref.shape[0]
    q = jnp.dot(h_ref[...], wq_ref[...], preferred_element_type=F32).astype(BF16)
    for piece in range(2 * PEER_HEADS):
        q_sc[piece] = q[:, piece * PEER_HALF:(piece + 1) * PEER_HALF]
    key_idx = lax.broadcasted_iota(jnp.int32, (PEER_N_KEYS, tm), 0).astype(F32)
    sub8 = lax.broadcasted_iota(jnp.int32, (8, tm), 0).astype(F32)
    neg = jnp.full((8, tm), -jnp.inf, F32)

    def head(h, carry):
        for side, (v_sc, i_sc) in enumerate(((v1_sc, i1_sc), (v2_sc, i2_sc))):
            qs = q_sc[2 * h + side]
            st = lax.dot_general(sk_ref[2 * h + side], qs, _NT, preferred_element_type=F32)
            _extract_topk(st, key_idx, v_sc, i_sc)
        v2lo = v2_sc[0:8, :]
        cands = [v1_sc[0:1, :] + v2lo, v1_sc[0:1, :] + v2_sc[8:16, :]]
        labels = [sub8, sub8 + 8.0]
        for a in range(1, 8):
            nb = PEER_TOPK // (a + 1)
            cands.append(jnp.where(sub8 < nb, v1_sc[a:a + 1, :] + v2lo, neg))
            labels.append(sub8 + float(a * PEER_TOPK))
        cands.append(v1_sc[8:16, :] + v2_sc[0:1, :])
        labels.append((sub8 + 8.0) * float(PEER_TOPK))
        cand = jnp.concatenate(cands, axis=0)
        label = jnp.concatenate(labels, axis=0)
        _extract_topk(cand, label, top_sc, lab_sc)
        top = top_sc[...]
        lab = lab_sc[...]
        a_sel = jnp.floor(lab * (1.0 / PEER_TOPK))
        b_sel = lab - a_sel * PEER_TOPK
        e1 = jnp.zeros_like(top)
        e2 = jnp.zeros_like(top)
        for r in range(PEER_TOPK):
            e1 = jnp.where(a_sel == float(r), i1_sc[r:r + 1, :], e1)
            e2 = jnp.where(b_sel == float(r), i2_sc[r:r + 1, :], e2)
        p = jnp.exp(top - jnp.max(top, axis=0, keepdims=True))
        gate = p / jnp.sum(p, axis=0, keepdims=True)
        row = pl.multiple_of(h * PEER_TOPK, PEER_TOPK)
        e1t_sc[pl.ds(row, PEER_TOPK), :] = e1
        e2t_sc[pl.ds(row, PEER_TOPK), :] = e2
        gt_sc[pl.ds(row, PEER_TOPK), :] = gate
        return carry

    lax.fori_loop(0, PEER_HEADS, head, 0)
    e1_ref[...] = e1t_sc[...].T
    e2_ref[...] = e2t_sc[...].T
    g_ref[...] = gt_sc[...].T


def _peer_topk(h2, wq, sk):
    t, d = h2.shape
    tm = min(512, t)
    nsel = PEER_HEADS * PEER_TOPK
    out = jax.ShapeDtypeStruct((t, nsel), F32)
    spec = pl.BlockSpec((tm, nsel), lambda i: (i, 0))
    small = lambda: pltpu.VMEM((PEER_TOPK, tm), F32)
    big = lambda: pltpu.VMEM((nsel, tm), F32)
    return pl.pallas_call(
        _peer_topk_kernel,
        out_shape=(out, out, out),
        grid=(t // tm,),
        in_specs=[pl.BlockSpec((tm, d), lambda i: (i, 0)),
                  pl.BlockSpec(wq.shape, lambda i: (0, 0)),
                  pl.BlockSpec(sk.shape, lambda i: (0, 0, 0))],
        out_specs=(spec, spec, spec),
        scratch_shapes=[pltpu.VMEM((2 * PEER_HEADS, tm, PEER_HALF), BF16),
                        small(), small(), small(), small(), small(), small(), big(), big(), big()],
        compiler_params=_cparams(1),
        name="peer_topk",
    )(h2, wq, sk)


def _peer_gates_kernel(e1_ref, e2_ref, g_ref, o_ref, g_sc, *, pitch):
    tm = e1_ref.shape[0]
    row_idx = lax.broadcasted_iota(jnp.int32, (PEER_N_KEYS, e1_ref.shape[1]), 0).astype(F32)

    def token(t, carry):
        r1 = e1_ref[pl.ds(t, 1), :]
        r2 = e2_ref[pl.ds(t, 1), :]
        gt = g_ref[pl.ds(t, 1), :]
        sel_i = jnp.where(row_idx == r1, 1.0, 0.0).astype(BF16)
        sel_j = jnp.where(row_idx == r2, gt, 0.0).astype(BF16)
        gm = lax.dot_general(sel_i, sel_j, _NT, preferred_element_type=F32)
        g_sc[pl.ds(t, PEER_N_KEYS, stride=pitch), :] = gm
        return carry

    lax.fori_loop(0, tm, token, 0, unroll=32)

    def emit(i, carry):
        start = pl.multiple_of(i * pitch, 8)
        o_ref[i] = g_sc[pl.ds(start, tm), :].astype(BF16)
        return carry

    lax.fori_loop(0, PEER_N_KEYS, emit, 0)


def _peer_gates(e1, e2, gate):
    t, nsel = e1.shape
    tm = min(128, t)
    pitch = tm + 8
    spec = pl.BlockSpec((tm, nsel), lambda i: (i, 0))
    return pl.pallas_call(
        functools.partial(_peer_gates_kernel, pitch=pitch),
        out_shape=jax.ShapeDtypeStruct((PEER_N_KEYS, t, PEER_N_KEYS), BF16),
        grid=(t // tm,),
        in_specs=[spec, spec, spec],
        out_specs=pl.BlockSpec((PEER_N_KEYS, tm, PEER_N_KEYS), lambda i: (0, i, 0)),
        scratch_shapes=[pltpu.VMEM((PEER_N_KEYS * pitch, PEER_N_KEYS), F32)],
        compiler_params=_cparams(1),
        name="peer_gate_matrix",
    )(e1, e2, gate)


def _peer_dense_kernel(h_ref, u_ref, v_ref, g_ref, x1_ref, mod_ref, o_ref, acc_sc):
    c = pl.program_id(1)

    @pl.when(c == 0)
    def _():
        acc_sc[...] = jnp.zeros_like(acc_sc)

    a = lax.dot_general(h_ref[...], u_ref[...], _NT, preferred_element_type=F32)
    act = 0.5 * a * (1.0 + lax.erf(a * (1.0 / math.sqrt(2.0))))
    ws = [(act[:, ib * LANES:(ib + 1) * LANES] * g_ref[ib].astype(F32)).astype(BF16)
          for ib in range(g_ref.shape[0])]
    w = jnp.concatenate(ws, axis=1)
    acc_sc[...] += jnp.dot(w, v_ref[...], preferred_element_type=F32)

    @pl.when(c == pl.num_programs(1) - 1)
    def _():
        o_ref[...] = x1_ref[...] + mod_ref[0][5:6] * acc_sc[...]


def _peer_dense(h2, u, v, gmat, x1, mod3, seq):
    t, d = h2.shape
    n_exp = u.shape[0]
    tm = min(1024, seq)
    spt = seq // tm
    tn = 1024
    return pl.pallas_call(
        _peer_dense_kernel,
        out_shape=jax.ShapeDtypeStruct((t, d), F32),
        grid=(t // tm, n_exp // tn),
        in_specs=[pl.BlockSpec((tm, d), lambda i, c: (i, 0)),
                  pl.BlockSpec((tn, d), lambda i, c: (c, 0)),
                  pl.BlockSpec((tn, d), lambda i, c: (c, 0)),
                  pl.BlockSpec((tn // PEER_N_KEYS, tm, PEER_N_KEYS), lambda i, c: (c, i, 0)),
                  pl.BlockSpec((tm, d), lambda i, c: (i, 0)),
                  pl.BlockSpec((1, 6, d), lambda i, c: (i // spt, 0, 0))],
        out_specs=pl.BlockSpec((tm, d), lambda i, c: (i, 0)),
        scratch_shapes=[pltpu.VMEM((tm, d), F32)],
        compiler_params=_cparams(2),
        name="peer_dense",
    )(h2, u, v, gmat, x1, mod3)


def _layer(x, c, cos, sin, lam_init, w_ada, b_ada, norm1_g, w_in, b_gate, qn_a, kn_a, w_proj_a,
           qn_b, kn_b, lam_q1, lam_k1, lam_q2, lam_k2, subln_g, w_proj_b, w_out, norm2_g,
           w_query, sub_keys, expert_u, expert_v):
    bsz, seq, d = x.shape
    t = bsz * seq
    x2 = x.reshape(t, d)
    mod3 = _ada(c, w_ada, b_ada).reshape(bsz, 6, d)
    gmat = _group_sum_matrix()
    scale = HEAD_DIM ** -0.5

    dil_w = len(DIL_GROUPS) * DIL_GROUP_WIDTH
    diff_w = DIFF_HEADS * 2 * HEAD_DIM
    a_cols = 3 * dil_w
    w_in_bf = w_in.astype(BF16)
    ones = lambda n: jnp.ones((n,), F32)
    gain_a = jnp.concatenate([jnp.tile(qn_a, dil_w // HEAD_DIM) * scale, jnp.tile(kn_a, dil_w // HEAD_DIM),
                              ones(dil_w)]).reshape(1, a_cols)
    qkv_groups = _inproj_a(x2, mod3, norm1_g.reshape(1, d), w_in_bf[:, :a_cols], gain_a, cos, sin, gmat, bsz, seq)

    n_b = w_in.shape[1] - a_cols
    gain_b = jnp.concatenate([jnp.tile(qn_b, diff_w // HEAD_DIM) * (scale * LOG2E),
                              jnp.tile(kn_b, diff_w // HEAD_DIM), ones(n_b - 2 * diff_w)]).reshape(1, n_b)
    bias_b = jnp.concatenate([jnp.zeros((3 * diff_w,), F32), b_gate]).reshape(1, n_b)
    qk_b, vt_b, gates = _inproj_b(x2, mod3, norm1_g.reshape(1, d), w_in_bf[:, a_cols:], gain_b, bias_b,
                                  cos, sin, gmat, seq, diff_w, diff_w)

    outs, lses = [], []
    for qkv in qkv_groups:
        o, l = _dilated_attn(qkv)
        outs.append(o)
        lses.append(l)
    oa = _dil_combine(outs, lses, bsz, seq)

    lam_vecs = [v.reshape(1, HEAD_DIM) for v in (lam_q1, lam_k1, lam_q2, lam_k2)]
    score_cap = (HEAD_DIM * scale * LOG2E * 1.01) * jnp.max(jnp.abs(qn_b)) * jnp.max(jnp.abs(kn_b))
    score_bounded = (score_cap <= DIFF_SCORE_BOUND).astype(jnp.int32).reshape(1)
    ob = _diff_attn(qk_b, vt_b, score_bounded, lam_vecs, subln_g.reshape(1, 2 * HEAD_DIM), bsz, seq, lam_init)

    x1, h2 = _merge(oa, ob, gates, x2, mod3, norm2_g.reshape(1, d), w_proj_a.astype(BF16),
                    w_proj_b.astype(BF16), w_out.astype(BF16), seq)

    sk = sub_keys.astype(BF16).reshape(PEER_HEADS * 2, PEER_N_KEYS, PEER_HALF)
    e1, e2, gate = _peer_topk(h2, w_query.astype(BF16), sk)
    gdense = _peer_gates(e1, e2, gate)
    out = _peer_dense(h2, expert_u.astype(BF16), expert_v.astype(BF16), gdense, x1, mod3, seq)
    return out.reshape(bsz, seq, d)


def kernel(x, c, positions, w_ada, b_ada, norm1_g, w_in, b_gate, qn_a, kn_a, w_proj_a, qn_b, kn_b,
           lam_q1, lam_k1, lam_q2, lam_k2, subln_g, w_proj_b, w_out, norm2_g, w_query, sub_keys,
           expert_u, expert_v):
    cos, sin = _rope_tables(positions)
    for l in range(w_ada.shape[0]):
        lam_init = 0.8 - 0.6 * math.exp(-0.3 * l)
        x = _layer(x, c, cos, sin, lam_init, w_ada[l], b_ada[l], norm1_g[l], w_in[l], b_gate[l],
                   qn_a[l], kn_a[l], w_proj_a[l], qn_b[l], kn_b[l], lam_q1[l], lam_k1[l], lam_q2[l],
                   lam_k2[l], subln_g[l], w_proj_b[l], w_out[l], norm2_g[l], w_query[l], sub_keys[l],
                   expert_u[l], expert_v[l])
    return x
```

```python
import functools
import math

import jax
import jax.numpy as jnp
from jax import lax
from jax.experimental import pallas as pl
from jax.experimental.pallas import tpu as pltpu

F32 = jnp.float32
BF16 = jnp.bfloat16

HEAD_DIM = 64
ROPE_THETA = 10000.0
EPS = 1e-6
NEG_INF = -1e30

DIL_GROUPS = ((128, 1), (512, 4), (2048, 16))
DIL_HEADS_PER_GROUP = 4
DIL_GROUP_WIDTH = DIL_HEADS_PER_GROUP * HEAD_DIM
DIL_SIDE = 64

DIFF_HEADS = 8
DIFF_TK = 1024
LOG2E = 1.4426950408889634
DIFF_SCORE_BOUND = 60.0
PEER_HEADS = 8
PEER_N_KEYS = 128
PEER_TOPK = 16
PEER_HALF = 128

LANES = 128
VMEM_LIMIT = 56 * 1024 * 1024

_NT = (((1,), (1,)), ((), ()))


def _cparams(n_axes):
    return pltpu.CompilerParams(dimension_semantics=("arbitrary",) * n_axes,
                                vmem_limit_bytes=VMEM_LIMIT)


def _ada_kernel(c_ref, w_ref, b_ref, o_ref):
    c = c_ref[...]
    sc = (c * jax.nn.sigmoid(c)).astype(BF16)
    o_ref[...] = jnp.dot(sc, w_ref[...].astype(BF16), preferred_element_type=F32) + b_ref[...]


def _ada(c, w_ada, b_ada):
    bsz, d = c.shape
    n = w_ada.shape[1]
    tn = 1024
    return pl.pallas_call(
        _ada_kernel,
        out_shape=jax.ShapeDtypeStruct((bsz, n), F32),
        grid=(n // tn,),
        in_specs=[pl.BlockSpec((bsz, d), lambda j: (0, 0)),
                  pl.BlockSpec((d, tn), lambda j: (0, j)),
                  pl.BlockSpec((1, tn), lambda j: (0, j))],
        out_specs=pl.BlockSpec((bsz, tn), lambda j: (0, j)),
        compiler_params=_cparams(1),
        name="ada_mod",
    )(c, w_ada, b_ada.reshape(1, n))


def _rope_kernel(pos_ref, freq_ref, cos_ref, sin_ref):
    ang = pos_ref[...] * freq_ref[...]
    lane = lax.broadcasted_iota(jnp.int32, ang.shape, 1)
    first_half = (lane % HEAD_DIM) < (HEAD_DIM // 2)
    cos_ref[...] = jnp.cos(ang)
    s = jnp.sin(ang)
    sin_ref[...] = jnp.where(first_half, -s, s)


def _rope_tables(positions):
    t = positions.size
    tm = min(1024, t)
    pos = positions.reshape(t, 1).astype(F32)
    inv_freq = 1.0 / (ROPE_THETA ** (jnp.arange(0, HEAD_DIM, 2, dtype=F32) / HEAD_DIM))
    freq = jnp.tile(inv_freq, LANES // (HEAD_DIM // 2)).reshape(1, LANES)
    return pl.pallas_call(
        _rope_kernel,
        out_shape=(jax.ShapeDtypeStruct((t, LANES), F32), jax.ShapeDtypeStruct((t, LANES), F32)),
        grid=(t // tm,),
        in_specs=[pl.BlockSpec((tm, 1), lambda i: (i, 0)),
                  pl.BlockSpec((1, LANES), lambda i: (0, 0))],
        out_specs=(pl.BlockSpec((tm, LANES), lambda i: (i, 0)),
                   pl.BlockSpec((tm, LANES), lambda i: (i, 0))),
        compiler_params=_cparams(1),
        name="rope_tables",
    )(pos, freq)


def _modulated_norm(x, g, scale, shift):
    ms = jnp.mean(x * x, axis=-1, keepdims=True)
    return (x * lax.rsqrt(ms + EPS)) * g * (1.0 + scale) + shift


def _head_norm_rope(acc, gain, cos, sin, gmat):
    ss = jnp.dot((acc * acc).astype(BF16), gmat, preferred_element_type=F32)
    yn = acc * lax.rsqrt(ss * (1.0 / HEAD_DIM) + EPS) * gain
    lane = lax.broadcasted_iota(jnp.int32, cos.shape, 1)
    first_half = (lane % HEAD_DIM) < (HEAD_DIM // 2)
    outs = []
    for hf in range(acc.shape[1] // LANES):
        y = yn[:, hf * LANES:(hf + 1) * LANES]
        up = pltpu.roll(y, HEAD_DIM // 2, axis=1)
        down = pltpu.roll(y, LANES - HEAD_DIM // 2, axis=1)
        swapped = jnp.where(first_half, down, up)
        outs.append(y * cos + swapped * sin)
    return jnp.concatenate(outs, axis=1)


def _group_sum_matrix():
    r = lax.broadcasted_iota(jnp.int32, (2 * LANES, 2 * LANES), 0) // HEAD_DIM
    c = lax.broadcasted_iota(jnp.int32, (2 * LANES, 2 * LANES), 1) // HEAD_DIM
    return (r == c).astype(BF16)


def _inproj_a_kernel(x_ref, mod_ref, ng_ref, w_ref, gain_ref, cos_ref, sin_ref, gmat_ref,
                     o0_ref, o1_ref, o2_ref, h_sc, y_sc):
    j = pl.program_id(1)
    tm = x_ref.shape[0]

    @pl.when(j == 0)
    def _():
        m = mod_ref[0]
        h_sc[...] = _modulated_norm(x_ref[...], ng_ref[...], m[1:2], m[0:1]).astype(BF16)

    acc = jnp.dot(h_sc[...], w_ref[...], preferred_element_type=F32)
    outs = (o0_ref, o1_ref, o2_ref)

    def emit(get_piece):
        for g, (_, dil) in enumerate(DIL_GROUPS):
            y = get_piece(g)
            if dil == 1:
                outs[g][0, 0, 0] = y.astype(BF16)
            else:
                rows = tm // dil
                for hf in range(DIL_GROUP_WIDTH // LANES):
                    y_sc[hf] = y[:, hf * LANES:(hf + 1) * LANES]
                for r in range(dil):
                    piece = jnp.concatenate(
                        [y_sc[hf, pl.ds(r, rows, stride=dil), :] for hf in range(DIL_GROUP_WIDTH // LANES)],
                        axis=1)
                    outs[g][0, 0, r] = piece.astype(BF16)

    @pl.when(j < 2)
    def _():
        cos = cos_ref[...]
        sin = sin_ref[...]
        gmat = gmat_ref[...]
        emit(lambda g: _head_norm_rope(acc[:, g * DIL_GROUP_WIDTH:(g + 1) * DIL_GROUP_WIDTH],
                                       gain_ref[:, g * DIL_GROUP_WIDTH:(g + 1) * DIL_GROUP_WIDTH],
                                       cos, sin, gmat))

    @pl.when(j == 2)
    def _():
        emit(lambda g: acc[:, g * DIL_GROUP_WIDTH:(g + 1) * DIL_GROUP_WIDTH])


def _inproj_a(x2, mod3, norm_g, w_a, gain_a, cos, sin, gmat, bsz, seq):
    t, d = x2.shape
    tm = min(1024, seq)
    spt = seq // tm
    n = w_a.shape[1]
    tn = n // 3
    out_shapes, out_specs = [], []
    for _, dil in DIL_GROUPS:
        out_shapes.append(jax.ShapeDtypeStruct((3, bsz, dil, seq // dil, DIL_GROUP_WIDTH), BF16))
        out_specs.append(pl.BlockSpec((1, 1, dil, tm // dil, DIL_GROUP_WIDTH),
                                      lambda i, j: (j, i // spt, 0, i % spt, 0)))
    return pl.pallas_call(
        _inproj_a_kernel,
        out_shape=tuple(out_shapes),
        grid=(t // tm, 3),
        in_specs=[pl.BlockSpec((tm, d), lambda i, j: (i, 0)),
                  pl.BlockSpec((1, 6, d), lambda i, j: (i // spt, 0, 0)),
                  pl.BlockSpec((1, d), lambda i, j: (0, 0)),
                  pl.BlockSpec((d, tn), lambda i, j: (0, j)),
                  pl.BlockSpec((1, tn), lambda i, j: (0, j)),
                  pl.BlockSpec((tm, LANES), lambda i, j: (i, 0)),
                  pl.BlockSpec((tm, LANES), lambda i, j: (i, 0)),
                  pl.BlockSpec((2 * LANES, 2 * LANES), lambda i, j: (0, 0))],
        out_specs=tuple(out_specs),
        scratch_shapes=[pltpu.VMEM((tm, d), BF16),
                        pltpu.VMEM((DIL_GROUP_WIDTH // LANES, tm, LANES), F32)],
        compiler_params=_cparams(2),
        name="inproj_dilated",
    )(x2, mod3, norm_g, w_a, gain_a, cos, sin, gmat)


def _inproj_b_kernel(x_ref, mod_ref, ng_ref, w_ref, gain_ref, bias_ref, cos_ref, sin_ref, gmat_ref,
                     qk_ref, vt_ref, gate_ref, h_sc, *, n_rope_tiles, n_plain_tiles):
    j = pl.program_id(1)

    @pl.when(j == 0)
    def _():
        m = mod_ref[0]
        h_sc[...] = _modulated_norm(x_ref[...], ng_ref[...], m[1:2], m[0:1]).astype(BF16)

    acc = jnp.dot(h_sc[...], w_ref[...], preferred_element_type=F32)
    tn = acc.shape[1]

    @pl.when(j < n_rope_tiles)
    def _():
        cos = cos_ref[...]
        sin = sin_ref[...]
        gmat = gmat_ref[...]
        for p in range(tn // (2 * LANES)):
            sl = slice(p * 2 * LANES, (p + 1) * 2 * LANES)
            qk_ref[:, sl] = _head_norm_rope(acc[:, sl], gain_ref[:, sl], cos, sin, gmat).astype(BF16)

    @pl.when((j >= n_rope_tiles) & (j < n_rope_tiles + n_plain_tiles))
    def _():
        for cc in range(vt_ref.shape[0]):
            vt_ref[cc] = acc[cc * DIFF_TK:(cc + 1) * DIFF_TK, :].T.astype(BF16)

    @pl.when(j >= n_rope_tiles + n_plain_tiles)
    def _():
        gate_ref[...] = jax.nn.sigmoid(acc + bias_ref[...]).astype(BF16)


def _inproj_b(x2, mod3, norm_g, w_b, gain_b, bias_b, cos, sin, gmat, seq, qk_width, v_width):
    t, d = x2.shape
    tm = min(1024, seq)
    spt = seq // tm
    n = w_b.shape[1]
    tn = 512
    n_rope, n_plain = 2 * qk_width // tn, v_width // tn
    n_gate = n // tn - n_rope - n_plain
    kern = functools.partial(_inproj_b_kernel, n_rope_tiles=n_rope, n_plain_tiles=n_plain)
    return pl.pallas_call(
        kern,
        out_shape=(jax.ShapeDtypeStruct((t, n_rope * tn), BF16),
                   jax.ShapeDtypeStruct((t // DIFF_TK, v_width, DIFF_TK), BF16),
                   jax.ShapeDtypeStruct((t, n_gate * tn), BF16)),
        grid=(t // tm, n // tn),
        in_specs=[pl.BlockSpec((tm, d), lambda i, j: (i, 0)),
                  pl.BlockSpec((1, 6, d), lambda i, j: (i // spt, 0, 0)),
                  pl.BlockSpec((1, d), lambda i, j: (0, 0)),
                  pl.BlockSpec((d, tn), lambda i, j: (0, j)),
                  pl.BlockSpec((1, tn), lambda i, j: (0, j)),
                  pl.BlockSpec((1, tn), lambda i, j: (0, j)),
                  pl.BlockSpec((tm, LANES), lambda i, j: (i, 0)),
                  pl.BlockSpec((tm, LANES), lambda i, j: (i, 0)),
                  pl.BlockSpec((2 * LANES, 2 * LANES), lambda i, j: (0, 0))],
        out_specs=(pl.BlockSpec((tm, tn), lambda i, j: (i, jnp.minimum(j, n_rope - 1))),
                   pl.BlockSpec((tm // DIFF_TK, tn, DIFF_TK),
                                lambda i, j: (i, jnp.clip(j - n_rope, 0, n_plain - 1), 0)),
                   pl.BlockSpec((tm, tn), lambda i, j: (i, jnp.clip(j - n_rope - n_plain, 0, n_gate - 1)))),
        scratch_shapes=[pltpu.VMEM((tm, d), BF16)],
        compiler_params=_cparams(2),
        name="inproj_diff_gates",
    )(x2, mod3, norm_g, w_b, gain_b, bias_b, cos, sin, gmat)


def _dilated_attn_kernel(q_ref, k_ref, v_ref, o_ref, lse_ref, *, tq, kw, length, n_sub):
    qi = pl.program_id(1)
    lane_head = lax.broadcasted_iota(jnp.int32, (tq, DIL_GROUP_WIDTH), 1) // HEAD_DIM
    for sub in range(n_sub):
        rows = slice(sub * tq, (sub + 1) * tq)
        q0 = (qi * n_sub + sub) * tq
        kstart = jnp.clip(q0 - DIL_SIDE, 0, length - kw)
        kstart = pl.multiple_of(kstart, DIL_SIDE)
        q = q_ref[0, 0, rows, :]
        k = k_ref[0, 0, pl.ds(kstart, kw), :]
        v = v_ref[0, 0, pl.ds(kstart, kw), :]
        n_h = DIL_HEADS_PER_GROUP
        qs = jnp.concatenate([jnp.where(lane_head == h, q, jnp.zeros_like(q)) for h in range(n_h)], axis=0)
        s = lax.dot_general(qs, k, _NT, preferred_element_type=F32)
        row = lax.broadcasted_iota(jnp.int32, (n_h * tq, kw), 0)
        qidx = q0 + row % tq
        kidx = kstart + lax.broadcasted_iota(jnp.int32, (n_h * tq, kw), 1)
        s = jnp.where(jnp.abs(kidx - qidx) <= DIL_SIDE, s, NEG_INF)
        m = jnp.max(s, axis=-1, keepdims=True)
        p = jnp.exp(s - m)
        l = jnp.sum(p, axis=-1, keepdims=True)
        pb = p.astype(BF16)
        lse = m + jnp.log(l)
        out = jnp.zeros((tq, DIL_GROUP_WIDTH), F32)
        lse_full = jnp.zeros((tq, DIL_GROUP_WIDTH), F32)
        for h in range(n_h):
            in_head = lane_head == h
            hr = slice(h * tq, (h + 1) * tq)
            oh = jnp.dot(pb[hr], v, preferred_element_type=F32)
            out = jnp.where(in_head, oh / l[hr], out)
            lse_full = jnp.where(in_head, lse[hr], lse_full)
        o_ref[0, rows, :] = out.astype(BF16)
        lse_ref[0, rows, :] = lse_full


def _dilated_attn(qkv):
    _, bsz, dil, length, w = qkv.shape
    bd = bsz * dil
    qkv = qkv.reshape(3, bd, length, w)
    tq = min(128, length)
    kw = min(tq + 2 * DIL_SIDE, length)
    n_sub = min(4, length // tq)
    tb = tq * n_sub
    kern = functools.partial(_dilated_attn_kernel, tq=tq, kw=kw, length=length, n_sub=n_sub)
    return pl.pallas_call(
        kern,
        out_shape=(jax.ShapeDtypeStruct((bd, length, w), BF16), jax.ShapeDtypeStruct((bd, length, w), F32)),
        grid=(bd, length // tb),
        in_specs=[pl.BlockSpec((1, 1, tb, w), lambda b, i: (0, b, i, 0)),
                  pl.BlockSpec((1, 1, length, w), lambda b, i: (1, b, 0, 0)),
                  pl.BlockSpec((1, 1, length, w), lambda b, i: (2, b, 0, 0))],
        out_specs=(pl.BlockSpec((1, tb, w), lambda b, i: (b, i, 0)),
                   pl.BlockSpec((1, tb, w), lambda b, i: (b, i, 0))),
        compiler_params=_cparams(2),
        name="dilated_attn",
    )(qkv, qkv, qkv)


def _combine_dilated_groups(o_refs, l_refs, nat_sc, tm):
    n_half = DIL_GROUP_WIDTH // LANES

    def natural(ref, dil):
        if dil == 1:
            return ref[0, 0].astype(F32)
        rows = tm // dil
        for r in range(dil):
            blk = ref[0, r].astype(F32)
            for hf in range(n_half):
                nat_sc[hf, pl.ds(r, rows, stride=dil), :] = blk[:, hf * LANES:(hf + 1) * LANES]
        return jnp.concatenate([nat_sc[hf] for hf in range(n_half)], axis=1)

    dils = [d for _, d in DIL_GROUPS]
    lses = [natural(r, d) for r, d in zip(l_refs, dils)]
    m = jnp.maximum(jnp.maximum(lses[0], lses[1]), lses[2])
    ws = [jnp.exp(l - m) for l in lses]
    den = ws[0] + ws[1] + ws[2]
    num = jnp.zeros_like(den)
    for w, r, d in zip(ws, o_refs, dils):
        num = num + w * natural(r, d)
    return num / den


def _diff_attn_kernel(bounded_ref, q_ref, k_ref, vt_ref, lq1_ref, lk1_ref, lq2_ref, lk2_ref, sg_ref, o_ref,
                      l_sc, acc_sc, *, lam_init):
    tq = q_ref.shape[0]
    n_chunks = vt_ref.shape[0]
    q = q_ref[...]
    lane = lax.broadcasted_iota(jnp.int32, q.shape, 1)
    zero = jnp.zeros_like(q)
    qz = jnp.concatenate([jnp.where(lane < HEAD_DIM, q, zero), jnp.where(lane >= HEAD_DIM, q, zero)], axis=0)

    def scores(c):
        ks = pl.multiple_of(c * DIFF_TK, DIFF_TK)
        k = k_ref[pl.ds(ks, DIFF_TK), :]
        return lax.dot_general(k, qz, _NT, preferred_element_type=F32)

    @pl.when(bounded_ref[0] != 0)
    def _():
        l = jnp.zeros((1, 2 * tq), F32)
        acc = jnp.zeros((vt_ref.shape[1], 2 * tq), F32)
        for c in range(n_chunks):
            p = jnp.exp2(scores(c))
            l = l + jnp.sum(p, axis=0, keepdims=True)
            acc = acc + jnp.dot(vt_ref[c], p.astype(BF16), preferred_element_type=F32)
        l_sc[...] = l
        acc_sc[...] = acc

    @pl.when(bounded_ref[0] == 0)
    def _():
        def body(c, carry):
            m, l, acc = carry
            st = scores(c)
            m_new = jnp.maximum(m, jnp.max(st, axis=0, keepdims=True))
            alpha = jnp.exp2(m - m_new)
            p = jnp.exp2(st - m_new)
            l = alpha * l + jnp.sum(p, axis=0, keepdims=True)
            acc = alpha * acc + jnp.dot(vt_ref[c], p.astype(BF16), preferred_element_type=F32)
            return m_new, l, acc

        init = (jnp.full((1, 2 * tq), -jnp.inf, F32), jnp.zeros((1, 2 * tq), F32),
                jnp.zeros((vt_ref.shape[1], 2 * tq), F32))
        _, l, acc = lax.fori_loop(0, n_chunks, body, init)
        l_sc[...] = l
        acc_sc[...] = acc

    o = acc_sc[...] / l_sc[...]
    lam = (jnp.exp(jnp.sum(lq1_ref[...] * lk1_ref[...], axis=-1, keepdims=True))
           - jnp.exp(jnp.sum(lq2_ref[...] * lk2_ref[...], axis=-1, keepdims=True)) + lam_init)
    obt = o[:, :tq] - lam * o[:, tq:]
    ms = jnp.mean(obt * obt, axis=0, keepdims=True)
    obn = obt * lax.rsqrt(ms + EPS)
    o_ref[...] = (obn.T * (sg_ref[...] * (1.0 - lam_init))).astype(BF16)


def _diff_attn(qk, vt, score_bounded, lam_vecs, subln_g, bsz, seq, lam_init):
    t = qk.shape[0]
    tq = min(512, seq)
    qpt = seq // tq
    cps = seq // DIFF_TK
    hw = 2 * HEAD_DIM
    kern = functools.partial(_diff_attn_kernel, lam_init=lam_init)
    vec_spec = pl.BlockSpec((1, HEAD_DIM), lambda b, h, i: (0, 0))
    return pl.pallas_call(
        kern,
        out_shape=jax.ShapeDtypeStruct((t, DIFF_HEADS * hw), BF16),
        grid=(bsz, DIFF_HEADS, qpt),
        in_specs=[pl.BlockSpec(memory_space=pltpu.SMEM),
                  pl.BlockSpec((tq, hw), lambda b, h, i: (b * qpt + i, h)),
                  pl.BlockSpec((seq, hw), lambda b, h, i: (b, DIFF_HEADS + h)),
                  pl.BlockSpec((cps, hw, DIFF_TK), lambda b, h, i: (b, h, 0)),
                  vec_spec, vec_spec, vec_spec, vec_spec,
                  pl.BlockSpec((1, hw), lambda b, h, i: (0, 0))],
        out_specs=pl.BlockSpec((tq, hw), lambda b, h, i: (b * qpt + i, h)),
        scratch_shapes=[pltpu.VMEM((1, 2 * tq), F32), pltpu.VMEM((hw, 2 * tq), F32)],
        compiler_params=_cparams(3),
        name="diff_attn",
    )(score_bounded, qk, qk, vt, *lam_vecs, subln_g)


def _merge_kernel(o0_ref, l0_ref, o1_ref, l1_ref, o2_ref, l2_ref, ob_ref, ga_ref, gb_ref, x_ref, mod_ref,
                  ng_ref, wpa_ref, wpb_ref, wo_ref, x1_ref, h2_ref, nat_sc):
    oa = _combine_dilated_groups((o0_ref, o1_ref, o2_ref), (l0_ref, l1_ref, l2_ref), nat_sc, x_ref.shape[0])
    ba = jnp.dot(oa.astype(BF16), wpa_ref[...], preferred_element_type=F32)
    bb = jnp.dot(ob_ref[...], wpb_ref[...], preferred_element_type=F32)
    mixed = ga_ref[...].astype(F32) * ba + gb_ref[...].astype(F32) * bb
    mo = jnp.dot(mixed.astype(BF16), wo_ref[...], preferred_element_type=F32)
    m = mod_ref[0]
    x1 = x_ref[...] + m[2:3] * mo
    x1_ref[...] = x1
    h2_ref[...] = _modulated_norm(x1, ng_ref[...], m[4:5], m[3:4]).astype(BF16)


def _merge(dil_outs, dil_lses, ob, gates, x2, mod3, norm2_g, wpa, wpb, wo, bsz, seq):
    t, d = x2.shape
    tm = min(512, seq)
    spt = seq // tm
    full = lambda i: (0, 0)
    dil_specs, dil_args = [], []
    for (o, l), (_, dil) in zip(zip(dil_outs, dil_lses), DIL_GROUPS):
        shape4 = (bsz, dil, seq // dil, DIL_GROUP_WIDTH)
        spec = pl.BlockSpec((1, dil, tm // dil, DIL_GROUP_WIDTH), lambda i: (i // spt, 0, i % spt, 0))
        dil_specs += [spec, spec]
        dil_args += [o.reshape(shape4), l.reshape(shape4)]
    return pl.pallas_call(
        _merge_kernel,
        out_shape=(jax.ShapeDtypeStruct((t, d), F32), jax.ShapeDtypeStruct((t, d), BF16)),
        grid=(t // tm,),
        in_specs=dil_specs + [
                  pl.BlockSpec((tm, d), lambda i: (i, 0)),
                  pl.BlockSpec((tm, d), lambda i: (i, 0)),
                  pl.BlockSpec((tm, d), lambda i: (i, 1)),
                  pl.BlockSpec((tm, d), lambda i: (i, 0)),
                  pl.BlockSpec((1, 6, d), lambda i: (i // spt, 0, 0)),
                  pl.BlockSpec((1, d), full),
                  pl.BlockSpec(wpa.shape, full),
                  pl.BlockSpec(wpb.shape, full),
                  pl.BlockSpec(wo.shape, full)],
        out_specs=(pl.BlockSpec((tm, d), lambda i: (i, 0)), pl.BlockSpec((tm, d), lambda i: (i, 0))),
        scratch_shapes=[pltpu.VMEM((DIL_GROUP_WIDTH // LANES, tm, LANES), F32)],
        compiler_params=_cparams(1),
        name="merge_outproj_norm2",
    )(*dil_args, ob, gates, gates, x2, mod3, norm2_g, wpa, wpb, wo)


def _extract_topk(s, idx_f, val_sc, idx_sc):
    for k in range(PEER_TOPK):
        m = jnp.max(s, axis=0, keepdims=True)
        idx = jnp.min(jnp.where(s == m, idx_f, 1e9), axis=0, keepdims=True)
        val_sc[k:k + 1, :] = m
        idx_sc[k:k + 1, :] = idx
        s = jnp.where(idx_f == idx, -jnp.inf, s)


def _peer_topk_kernel(h_ref, wq_ref, sk_ref, e1_ref, e2_ref, g_ref,
                      q_sc, v1_sc, i1_sc, v2_sc, i2_sc, top_sc, lab_sc, e1t_sc, e2t_sc, gt_sc):
    tm = h_ref.shape[0]
    q = jnp.dot(h_ref[...], wq_ref[...], preferred_element_type=F32).astype(BF16)
    for piece in range(2 * PEER_HEADS):
        q_sc[piece] = q[:, piece * PEER_HALF:(piece + 1) * PEER_HALF]
    key_idx = lax.broadcasted_iota(jnp.int32, (PEER_N_KEYS, tm), 0).astype(F32)
    sub8 = lax.broadcasted_iota(jnp.int32, (8, tm), 0).astype(F32)
    neg = jnp.full((8, tm), -jnp.inf, F32)

    def head(h, carry):
        for side, (v_sc, i_sc) in enumerate(((v1_sc, i1_sc), (v2_sc, i2_sc))):
            qs = q_sc[2 * h + side]
            st = lax.dot_general(sk_ref[2 * h + side], qs, _NT, preferred_element_type=F32)
            _extract_topk(st, key_idx, v_sc, i_sc)
        v2lo = v2_sc[0:8, :]
        cands = [v1_sc[0:1, :] + v2lo, v1_sc[0:1, :] + v2_sc[8:16, :]]
        labels = [sub8, sub8 + 8.0]
        for a in range(1, 8):
            nb = PEER_TOPK // (a + 1)
            cands.append(jnp.where(sub8 < nb, v1_sc[a:a + 1, :] + v2lo, neg))
            labels.append(sub8 + float(a * PEER_TOPK))
        cands.append(v1_sc[8:16, :] + v2_sc[0:1, :])
        labels.append((sub8 + 8.0) * float(PEER_TOPK))
        cand = jnp.concatenate(cands, axis=0)
        label = jnp.concatenate(labels, axis=0)
        _extract_topk(cand, label, top_sc, lab_sc)
        top = top_sc[...]
        lab = lab_sc[...]
        a_sel = jnp.floor(lab * (1.0 / PEER_TOPK))
        b_sel = lab - a_sel * PEER_TOPK
        e1 = jnp.zeros_like(top)
        e2 = jnp.zeros_like(top)
        for r in range(PEER_TOPK):
            e1 = jnp.where(a_sel == float(r), i1_sc[r:r + 1, :], e1)
            e2 = jnp.where(b_sel == float(r), i2_sc[r:r + 1, :], e2)
        p = jnp.exp(top - jnp.max(top, axis=0, keepdims=True))
        gate = p / jnp.sum(p, axis=0, keepdims=True)
        row = pl.multiple_of(h * PEER_TOPK, PEER_TOPK)
        e1t_sc[pl.ds(row, PEER_TOPK), :] = e1
        e2t_sc[pl.ds(row, PEER_TOPK), :] = e2
        gt_sc[pl.ds(row, PEER_TOPK), :] = gate
        return carry

    lax.fori_loop(0, PEER_HEADS, head, 0)
    e1_ref[...] = e1t_sc[...].T
    e2_ref[...] = e2t_sc[...].T
    g_ref[...] = gt_sc[...].T


def _peer_topk(h2, wq, sk):
    t, d = h2.shape
    tm = min(512, t)
    nsel = PEER_HEADS * PEER_TOPK
    out = jax.ShapeDtypeStruct((t, nsel), F32)
    spec = pl.BlockSpec((tm, nsel), lambda i: (i, 0))
    small = lambda: pltpu.VMEM((PEER_TOPK, tm), F32)
    big = lambda: pltpu.VMEM((nsel, tm), F32)
    return pl.pallas_call(
        _peer_topk_kernel,
        out_shape=(out, out, out),
        grid=(t // tm,),
        in_specs=[pl.BlockSpec((tm, d), lambda i: (i, 0)),
                  pl.BlockSpec(wq.shape, lambda i: (0, 0)),
                  pl.BlockSpec(sk.shape, lambda i: (0, 0, 0))],
        out_specs=(spec, spec, spec),
        scratch_shapes=[pltpu.VMEM((2 * PEER_HEADS, tm, PEER_HALF), BF16),
                        small(), small(), small(), small(), small(), small(), big(), big(), big()],
        compiler_params=_cparams(1),
        name="peer_topk",
    )(h2, wq, sk)


def _peer_gates_kernel(e1_ref, e2_ref, g_ref, o_ref, g_sc, *, pitch):
    tm = e1_ref.shape[0]
    row_idx = lax.broadcasted_iota(jnp.int32, (PEER_N_KEYS, e1_ref.shape[1]), 0).astype(F32)

    def token(t, carry):
        r1 = e1_ref[pl.ds(t, 1), :]
        r2 = e2_ref[pl.ds(t, 1), :]
        gt = g_ref[pl.ds(t, 1), :]
        sel_i = jnp.where(row_idx == r1, 1.0, 0.0).astype(BF16)
        sel_j = jnp.where(row_idx == r2, gt, 0.0).astype(BF16)
        gm = lax.dot_general(sel_i, sel_j, _NT, preferred_element_type=F32)
        g_sc[pl.ds(t, PEER_N_KEYS, stride=pitch), :] = gm
        return carry

    lax.fori_loop(0, tm, token, 0, unroll=32)

    def emit(i, carry):
        start = pl.multiple_of(i * pitch, 8)
        o_ref[i] = g_sc[pl.ds(start, tm), :].astype(BF16)
        return carry

    lax.fori_loop(0, PEER_N_KEYS, emit, 0)


def _peer_gates(e1, e2, gate):
    t, nsel = e1.shape
    tm = min(128, t)
    pitch = tm + 8
    spec = pl.BlockSpec((tm, nsel), lambda i: (i, 0))
    return pl.pallas_call(
        functools.partial(_peer_gates_kernel, pitch=pitch),
        out_shape=jax.ShapeDtypeStruct((PEER_N_KEYS, t, PEER_N_KEYS), BF16),
        grid=(t // tm,),
        in_specs=[spec, spec, spec],
        out_specs=pl.BlockSpec((PEER_N_KEYS, tm, PEER_N_KEYS), lambda i: (0, i, 0)),
        scratch_shapes=[pltpu.VMEM((PEER_N_KEYS * pitch, PEER_N_KEYS), F32)],
        compiler_params=_cparams(1),
        name="peer_gate_matrix",
    )(e1, e2, gate)


def _peer_dense_kernel(h_ref, u_ref, v_ref, g_ref, x1_ref, mod_ref, o_ref, acc_sc):
    c = pl.program_id(1)

    @pl.when(c == 0)
    def _():
        acc_sc[...] = jnp.zeros_like(acc_sc)

    a = lax.dot_general(h_ref[...], u_ref[...], _NT, preferred_element_type=F32)
    act = 0.5 * a * (1.0 + lax.erf(a * (1.0 / math.sqrt(2.0))))
    ws = [(act[:, ib * LANES:(ib + 1) * LANES] * g_ref[ib].astype(F32)).astype(BF16)
          for ib in range(g_ref.shape[0])]
    w = jnp.concatenate(ws, axis=1)
    acc_sc[...] += jnp.dot(w, v_ref[...], preferred_element_type=F32)

    @pl.when(c == pl.num_programs(1) - 1)
    def _():
        o_ref[...] = x1_ref[...] + mod_ref[0][5:6] * acc_sc[...]


def _peer_dense(h2, u, v, gmat, x1, mod3, seq):
    t, d = h2.shape
    n_exp = u.shape[0]
    tm = min(1024, seq)
    spt = seq // tm
    tn = 1024
    return pl.pallas_call(
        _peer_dense_kernel,
        out_shape=jax.ShapeDtypeStruct((t, d), F32),
        grid=(t // tm, n_exp // tn),
        in_specs=[pl.BlockSpec((tm, d), lambda i, c: (i, 0)),
                  pl.BlockSpec((tn, d), lambda i, c: (c, 0)),
                  pl.BlockSpec((tn, d), lambda i, c: (c, 0)),
                  pl.BlockSpec((tn // PEER_N_KEYS, tm, PEER_N_KEYS), lambda i, c: (c, i, 0)),
                  pl.BlockSpec((tm, d), lambda i, c: (i, 0)),
                  pl.BlockSpec((1, 6, d), lambda i, c: (i // spt, 0, 0))],
        out_specs=pl.BlockSpec((tm, d), lambda i, c: (i, 0)),
        scratch_shapes=[pltpu.VMEM((tm, d), F32)],
        compiler_params=_cparams(2),
        name="peer_dense",
    )(h2, u, v, gmat, x1, mod3)


def _layer(x, c, cos, sin, lam_init, w_ada, b_ada, norm1_g, w_in, b_gate, qn_a, kn_a, w_proj_a,
           qn_b, kn_b, lam_q1, lam_k1, lam_q2, lam_k2, subln_g, w_proj_b, w_out, norm2_g,
           w_query, sub_keys, expert_u, expert_v):
    bsz, seq, d = x.shape
    t = bsz * seq
    x2 = x.reshape(t, d)
    mod3 = _ada(c, w_ada, b_ada).reshape(bsz, 6, d)
    gmat = _group_sum_matrix()
    scale = HEAD_DIM ** -0.5

    dil_w = len(DIL_GROUPS) * DIL_GROUP_WIDTH
    diff_w = DIFF_HEADS * 2 * HEAD_DIM
    a_cols = 3 * dil_w
    w_in_bf = w_in.astype(BF16)
    ones = lambda n: jnp.ones((n,), F32)
    gain_a = jnp.concatenate([jnp.tile(qn_a, dil_w // HEAD_DIM) * scale, jnp.tile(kn_a, dil_w // HEAD_DIM),
                              ones(dil_w)]).reshape(1, a_cols)
    qkv_groups = _inproj_a(x2, mod3, norm1_g.reshape(1, d), w_in_bf[:, :a_cols], gain_a, cos, sin, gmat, bsz, seq)

    n_b = w_in.shape[1] - a_cols
    gain_b = jnp.concatenate([jnp.tile(qn_b, diff_w // HEAD_DIM) * (scale * LOG2E),
                              jnp.tile(kn_b, diff_w // HEAD_DIM), ones(n_b - 2 * diff_w)]).reshape(1, n_b)
    bias_b = jnp.concatenate([jnp.zeros((3 * diff_w,), F32), b_gate]).reshape(1, n_b)
    qk_b, vt_b, gates = _inproj_b(x2, mod3, norm1_g.reshape(1, d), w_in_bf[:, a_cols:], gain_b, bias_b,
                                  cos, sin, gmat, seq, diff_w, diff_w)

    outs, lses = [], []
    for qkv in qkv_groups:
        o, l = _dilated_attn(qkv)
        outs.append(o)
        lses.append(l)

    lam_vecs = [v.reshape(1, HEAD_DIM) for v in (lam_q1, lam_k1, lam_q2, lam_k2)]
    score_cap = (HEAD_DIM * scale * LOG2E * 1.01) * jnp.max(jnp.abs(qn_b)) * jnp.max(jnp.abs(kn_b))
    score_bounded = (score_cap <= DIFF_SCORE_BOUND).astype(jnp.int32).reshape(1)
    ob = _diff_attn(qk_b, vt_b, score_bounded, lam_vecs, subln_g.reshape(1, 2 * HEAD_DIM), bsz, seq, lam_init)

    x1, h2 = _merge(outs, lses, ob, gates, x2, mod3, norm2_g.reshape(1, d), w_proj_a.astype(BF16),
                    w_proj_b.astype(BF16), w_out.astype(BF16), bsz, seq)

    sk = sub_keys.astype(BF16).reshape(PEER_HEADS * 2, PEER_N_KEYS, PEER_HALF)
    e1, e2, gate = _peer_topk(h2, w_query.astype(BF16), sk)
    gdense = _peer_gates(e1, e2, gate)
    out = _peer_dense(h2, expert_u.astype(BF16), expert_v.astype(BF16), gdense, x1, mod3, seq)
    return out.reshape(bsz, seq, d)


def kernel(x, c, positions, w_ada, b_ada, norm1_g, w_in, b_gate, qn_a, kn_a, w_proj_a, qn_b, kn_b,
           lam_q1, lam_k1, lam_q2, lam_k2, subln_g, w_proj_b, w_out, norm2_g, w_query, sub_keys,
           expert_u, expert_v):
    cos, sin = _rope_tables(positions)
    for l in range(w_ada.shape[0]):
        lam_init = 0.8 - 0.6 * math.exp(-0.3 * l)
        x = _layer(x, c, cos, sin, lam_init, w_ada[l], b_ada[l], norm1_g[l], w_in[l], b_gate[l],
                   qn_a[l], kn_a[l], w_proj_a[l], qn_b[l], kn_b[l], lam_q1[l], lam_k1[l], lam_q2[l],
                   lam_k2[l], subln_g[l], w_proj_b[l], w_out[l], norm2_g[l], w_query[l], sub_keys[l],
                   expert_u[l], expert_v[l])
    return x
```

```python
import functools
import math

import jax
import jax.numpy as jnp
from jax import lax
from jax.experimental import pallas as pl
from jax.experimental.pallas import tpu as pltpu

F32 = jnp.float32
BF16 = jnp.bfloat16

HEAD_DIM = 64
ROPE_THETA = 10000.0
EPS = 1e-6
NEG_INF = -1e30

DIL_GROUPS = ((128, 1), (512, 4), (2048, 16))
DIL_HEADS_PER_GROUP = 4
DIL_GROUP_WIDTH = DIL_HEADS_PER_GROUP * HEAD_DIM
DIL_SIDE = 64

DIFF_HEADS = 8
DIFF_TK = 1024
LOG2E = 1.4426950408889634
DIFF_SCORE_BOUND = 60.0
PEER_HEADS = 8
PEER_N_KEYS = 128
PEER_TOPK = 16
PEER_HALF = 128

LANES = 128
VMEM_LIMIT = 56 * 1024 * 1024

_NT = (((1,), (1,)), ((), ()))


def _cparams(n_axes):
    return pltpu.CompilerParams(dimension_semantics=("arbitrary",) * n_axes,
                                vmem_limit_bytes=VMEM_LIMIT)


def _ada_kernel(c_ref, w_ref, b_ref, o_ref):
    c = c_ref[...]
    sc = (c * jax.nn.sigmoid(c)).astype(BF16)
    o_ref[...] = jnp.dot(sc, w_ref[...].astype(BF16), preferred_element_type=F32) + b_ref[...]


def _ada(c, w_ada, b_ada):
    bsz, d = c.shape
    n = w_ada.shape[1]
    tn = 1024
    return pl.pallas_call(
        _ada_kernel,
        out_shape=jax.ShapeDtypeStruct((bsz, n), F32),
        grid=(n // tn,),
        in_specs=[pl.BlockSpec((bsz, d), lambda j: (0, 0)),
                  pl.BlockSpec((d, tn), lambda j: (0, j)),
                  pl.BlockSpec((1, tn), lambda j: (0, j))],
        out_specs=pl.BlockSpec((bsz, tn), lambda j: (0, j)),
        compiler_params=_cparams(1),
        name="ada_mod",
    )(c, w_ada, b_ada.reshape(1, n))


def _rope_kernel(pos_ref, freq_ref, cos_ref, sin_ref):
    ang = pos_ref[...] * freq_ref[...]
    lane = lax.broadcasted_iota(jnp.int32, ang.shape, 1)
    first_half = (lane % HEAD_DIM) < (HEAD_DIM // 2)
    cos_ref[...] = jnp.cos(ang)
    s = jnp.sin(ang)
    sin_ref[...] = jnp.where(first_half, -s, s)


def _rope_tables(positions):
    t = positions.size
    tm = min(1024, t)
    pos = positions.reshape(t, 1).astype(F32)
    inv_freq = 1.0 / (ROPE_THETA ** (jnp.arange(0, HEAD_DIM, 2, dtype=F32) / HEAD_DIM))
    freq = jnp.tile(inv_freq, LANES // (HEAD_DIM // 2)).reshape(1, LANES)
    return pl.pallas_call(
        _rope_kernel,
        out_shape=(jax.ShapeDtypeStruct((t, LANES), F32), jax.ShapeDtypeStruct((t, LANES), F32)),
        grid=(t // tm,),
        in_specs=[pl.BlockSpec((tm, 1), lambda i: (i, 0)),
                  pl.BlockSpec((1, LANES), lambda i: (0, 0))],
        out_specs=(pl.BlockSpec((tm, LANES), lambda i: (i, 0)),
                   pl.BlockSpec((tm, LANES), lambda i: (i, 0))),
        compiler_params=_cparams(1),
        name="rope_tables",
    )(pos, freq)


def _modulated_norm(x, g, scale, shift):
    ms = jnp.mean(x * x, axis=-1, keepdims=True)
    return (x * lax.rsqrt(ms + EPS)) * g * (1.0 + scale) + shift


def _head_norm_rope(acc, gain, cos, sin, gmat):
    ss = jnp.dot((acc * acc).astype(BF16), gmat, preferred_element_type=F32)
    yn = acc * lax.rsqrt(ss * (1.0 / HEAD_DIM) + EPS) * gain
    lane = lax.broadcasted_iota(jnp.int32, cos.shape, 1)
    first_half = (lane % HEAD_DIM) < (HEAD_DIM // 2)
    outs = []
    for hf in range(acc.shape[1] // LANES):
        y = yn[:, hf * LANES:(hf + 1) * LANES]
        up = pltpu.roll(y, HEAD_DIM // 2, axis=1)
        down = pltpu.roll(y, LANES - HEAD_DIM // 2, axis=1)
        swapped = jnp.where(first_half, down, up)
        outs.append(y * cos + swapped * sin)
    return jnp.concatenate(outs, axis=1)


def _group_sum_matrix():
    r = lax.broadcasted_iota(jnp.int32, (2 * LANES, 2 * LANES), 0) // HEAD_DIM
    c = lax.broadcasted_iota(jnp.int32, (2 * LANES, 2 * LANES), 1) // HEAD_DIM
    return (r == c).astype(BF16)


def _inproj_a_kernel(x_ref, mod_ref, ng_ref, w_ref, gain_ref, cos_ref, sin_ref, gmat_ref,
                     o0_ref, o1_ref, o2_ref, h_sc, y_sc):
    j = pl.program_id(1)
    tm = x_ref.shape[0]

    @pl.when(j == 0)
    def _():
        m = mod_ref[0]
        h_sc[...] = _modulated_norm(x_ref[...], ng_ref[...], m[1:2], m[0:1]).astype(BF16)

    acc = jnp.dot(h_sc[...], w_ref[...], preferred_element_type=F32)
    outs = (o0_ref, o1_ref, o2_ref)

    def emit(get_piece):
        for g, (_, dil) in enumerate(DIL_GROUPS):
            y = get_piece(g)
            if dil == 1:
                outs[g][0, 0, 0] = y.astype(BF16)
            else:
                rows = tm // dil
                for hf in range(DIL_GROUP_WIDTH // LANES):
                    y_sc[hf] = y[:, hf * LANES:(hf + 1) * LANES]
                for r in range(dil):
                    piece = jnp.concatenate(
                        [y_sc[hf, pl.ds(r, rows, stride=dil), :] for hf in range(DIL_GROUP_WIDTH // LANES)],
                        axis=1)
                    outs[g][0, 0, r] = piece.astype(BF16)

    @pl.when(j < 2)
    def _():
        cos = cos_ref[...]
        sin = sin_ref[...]
        gmat = gmat_ref[...]
        emit(lambda g: _head_norm_rope(acc[:, g * DIL_GROUP_WIDTH:(g + 1) * DIL_GROUP_WIDTH],
                                       gain_ref[:, g * DIL_GROUP_WIDTH:(g + 1) * DIL_GROUP_WIDTH],
                                       cos, sin, gmat))

    @pl.when(j == 2)
    def _():
        emit(lambda g: acc[:, g * DIL_GROUP_WIDTH:(g + 1) * DIL_GROUP_WIDTH])


def _inproj_a(x2, mod3, norm_g, w_a, gain_a, cos, sin, gmat, bsz, seq):
    t, d = x2.shape
    tm = min(1024, seq)
    spt = seq // tm
    n = w_a.shape[1]
    tn = n // 3
    out_shapes, out_specs = [], []
    for _, dil in DIL_GROUPS:
        out_shapes.append(jax.ShapeDtypeStruct((3, bsz, dil, seq // dil, DIL_GROUP_WIDTH), BF16))
        out_specs.append(pl.BlockSpec((1, 1, dil, tm // dil, DIL_GROUP_WIDTH),
                                      lambda i, j: (j, i // spt, 0, i % spt, 0)))
    return pl.pallas_call(
        _inproj_a_kernel,
        out_shape=tuple(out_shapes),
        grid=(t // tm, 3),
        in_specs=[pl.BlockSpec((tm, d), lambda i, j: (i, 0)),
                  pl.BlockSpec((1, 6, d), lambda i, j: (i // spt, 0, 0)),
                  pl.BlockSpec((1, d), lambda i, j: (0, 0)),
                  pl.BlockSpec((d, tn), lambda i, j: (0, j)),
                  pl.BlockSpec((1, tn), lambda i, j: (0, j)),
                  pl.BlockSpec((tm, LANES), lambda i, j: (i, 0)),
                  pl.BlockSpec((tm, LANES), lambda i, j: (i, 0)),
                  pl.BlockSpec((2 * LANES, 2 * LANES), lambda i, j: (0, 0))],
        out_specs=tuple(out_specs),
        scratch_shapes=[pltpu.VMEM((tm, d), BF16),
                        pltpu.VMEM((DIL_GROUP_WIDTH // LANES, tm, LANES), F32)],
        compiler_params=_cparams(2),
        name="inproj_dilated",
    )(x2, mod3, norm_g, w_a, gain_a, cos, sin, gmat)


def _inproj_b_kernel(x_ref, mod_ref, ng_ref, w_ref, gain_ref, bias_ref, cos_ref, sin_ref, gmat_ref,
                     qk_ref, vt_ref, gate_ref, h_sc, *, n_rope_tiles, n_plain_tiles):
    j = pl.program_id(1)

    @pl.when(j == 0)
    def _():
        m = mod_ref[0]
        h_sc[...] = _modulated_norm(x_ref[...], ng_ref[...], m[1:2], m[0:1]).astype(BF16)

    acc = jnp.dot(h_sc[...], w_ref[...], preferred_element_type=F32)
    tn = acc.shape[1]

    @pl.when(j < n_rope_tiles)
    def _():
        cos = cos_ref[...]
        sin = sin_ref[...]
        gmat = gmat_ref[...]
        for p in range(tn // (2 * LANES)):
            sl = slice(p * 2 * LANES, (p + 1) * 2 * LANES)
            qk_ref[:, sl] = _head_norm_rope(acc[:, sl], gain_ref[:, sl], cos, sin, gmat).astype(BF16)

    @pl.when((j >= n_rope_tiles) & (j < n_rope_tiles + n_plain_tiles))
    def _():
        for cc in range(vt_ref.shape[0]):
            vt_ref[cc] = acc[cc * DIFF_TK:(cc + 1) * DIFF_TK, :].T.astype(BF16)

    @pl.when(j >= n_rope_tiles + n_plain_tiles)
    def _():
        gate_ref[...] = jax.nn.sigmoid(acc + bias_ref[...]).astype(BF16)


def _inproj_b(x2, mod3, norm_g, w_b, gain_b, bias_b, cos, sin, gmat, seq, qk_width, v_width):
    t, d = x2.shape
    tm = min(1024, seq)
    spt = seq // tm
    n = w_b.shape[1]
    tn = 1024
    n_rope, n_plain = 2 * qk_width // tn, v_width // tn
    n_gate = n // tn - n_rope - n_plain
    kern = functools.partial(_inproj_b_kernel, n_rope_tiles=n_rope, n_plain_tiles=n_plain)
    return pl.pallas_call(
        kern,
        out_shape=(jax.ShapeDtypeStruct((t, n_rope * tn), BF16),
                   jax.ShapeDtypeStruct((t // DIFF_TK, v_width, DIFF_TK), BF16),
                   jax.ShapeDtypeStruct((t, n_gate * tn), BF16)),
        grid=(t // tm, n // tn),
        in_specs=[pl.BlockSpec((tm, d), lambda i, j: (i, 0)),
                  pl.BlockSpec((1, 6, d), lambda i, j: (i // spt, 0, 0)),
                  pl.BlockSpec((1, d), lambda i, j: (0, 0)),
                  pl.BlockSpec((d, tn), lambda i, j: (0, j)),
                  pl.BlockSpec((1, tn), lambda i, j: (0, j)),
                  pl.BlockSpec((1, tn), lambda i, j: (0, j)),
                  pl.BlockSpec((tm, LANES), lambda i, j: (i, 0)),
                  pl.BlockSpec((tm, LANES), lambda i, j: (i, 0)),
                  pl.BlockSpec((2 * LANES, 2 * LANES), lambda i, j: (0, 0))],
        out_specs=(pl.BlockSpec((tm, tn), lambda i, j: (i, jnp.minimum(j, n_rope - 1))),
                   pl.BlockSpec((tm // DIFF_TK, tn, DIFF_TK),
                                lambda i, j: (i, jnp.clip(j - n_rope, 0, n_plain - 1), 0)),
                   pl.BlockSpec((tm, tn), lambda i, j: (i, jnp.clip(j - n_rope - n_plain, 0, n_gate - 1)))),
        scratch_shapes=[pltpu.VMEM((tm, d), BF16)],
        compiler_params=_cparams(2),
        name="inproj_diff_gates",
    )(x2, mod3, norm_g, w_b, gain_b, bias_b, cos, sin, gmat)


def _dilated_attn_kernel(q_ref, k_ref, v_ref, o_ref, lse_ref, *, tq, kw, length, n_sub):
    qi = pl.program_id(1)
    lane_head = lax.broadcasted_iota(jnp.int32, (tq, DIL_GROUP_WIDTH), 1) // HEAD_DIM
    for sub in range(n_sub):
        rows = slice(sub * tq, (sub + 1) * tq)
        q0 = (qi * n_sub + sub) * tq
        kstart = jnp.clip(q0 - DIL_SIDE, 0, length - kw)
        kstart = pl.multiple_of(kstart, DIL_SIDE)
        q = q_ref[0, 0, rows, :]
        k = k_ref[0, 0, pl.ds(kstart, kw), :]
        v = v_ref[0, 0, pl.ds(kstart, kw), :]
        n_h = DIL_HEADS_PER_GROUP
        qs = jnp.concatenate([jnp.where(lane_head == h, q, jnp.zeros_like(q)) for h in range(n_h)], axis=0)
        s = lax.dot_general(qs, k, _NT, preferred_element_type=F32)
        row = lax.broadcasted_iota(jnp.int32, (n_h * tq, kw), 0)
        qidx = q0 + row % tq
        kidx = kstart + lax.broadcasted_iota(jnp.int32, (n_h * tq, kw), 1)
        s = jnp.where(jnp.abs(kidx - qidx) <= DIL_SIDE, s, NEG_INF)
        m = jnp.max(s, axis=-1, keepdims=True)
        p = jnp.exp(s - m)
        l = jnp.sum(p, axis=-1, keepdims=True)
        pb = p.astype(BF16)
        lse = m + jnp.log(l)
        out = jnp.zeros((tq, DIL_GROUP_WIDTH), F32)
        lse_full = jnp.zeros((tq, DIL_GROUP_WIDTH), F32)
        for h in range(n_h):
            in_head = lane_head == h
            hr = slice(h * tq, (h + 1) * tq)
            oh = jnp.dot(pb[hr], v, preferred_element_type=F32)
            out = jnp.where(in_head, oh / l[hr], out)
            lse_full = jnp.where(in_head, lse[hr], lse_full)
        o_ref[0, rows, :] = out.astype(BF16)
        lse_ref[0, rows, :] = lse_full


def _dilated_attn(qkv):
    _, bsz, dil, length, w = qkv.shape
    bd = bsz * dil
    qkv = qkv.reshape(3, bd, length, w)
    tq = min(128, length)
    kw = min(tq + 2 * DIL_SIDE, length)
    n_sub = min(4, length // tq)
    tb = tq * n_sub
    kern = functools.partial(_dilated_attn_kernel, tq=tq, kw=kw, length=length, n_sub=n_sub)
    return pl.pallas_call(
        kern,
        out_shape=(jax.ShapeDtypeStruct((bd, length, w), BF16), jax.ShapeDtypeStruct((bd, length, w), F32)),
        grid=(bd, length // tb),
        in_specs=[pl.BlockSpec((1, 1, tb, w), lambda b, i: (0, b, i, 0)),
                  pl.BlockSpec((1, 1, length, w), lambda b, i: (1, b, 0, 0)),
                  pl.BlockSpec((1, 1, length, w), lambda b, i: (2, b, 0, 0))],
        out_specs=(pl.BlockSpec((1, tb, w), lambda b, i: (b, i, 0)),
                   pl.BlockSpec((1, tb, w), lambda b, i: (b, i, 0))),
        compiler_params=_cparams(2),
        name="dilated_attn",
    )(qkv, qkv, qkv)


def _combine_dilated_groups(o_refs, l_refs, nat_sc, tm):
    n_half = DIL_GROUP_WIDTH // LANES

    def natural(ref, dil):
        if dil == 1:
            return ref[0, 0].astype(F32)
        rows = tm // dil
        for r in range(dil):
            blk = ref[0, r].astype(F32)
            for hf in range(n_half):
                nat_sc[hf, pl.ds(r, rows, stride=dil), :] = blk[:, hf * LANES:(hf + 1) * LANES]
        return jnp.concatenate([nat_sc[hf] for hf in range(n_half)], axis=1)

    dils = [d for _, d in DIL_GROUPS]
    lses = [natural(r, d) for r, d in zip(l_refs, dils)]
    m = jnp.maximum(jnp.maximum(lses[0], lses[1]), lses[2])
    ws = [jnp.exp(l - m) for l in lses]
    den = ws[0] + ws[1] + ws[2]
    num = jnp.zeros_like(den)
    for w, r, d in zip(ws, o_refs, dils):
        num = num + w * natural(r, d)
    return num / den


def _diff_attn_kernel(bounded_ref, q_ref, k_ref, vt_ref, lq1_ref, lk1_ref, lq2_ref, lk2_ref, sg_ref, o_ref,
                      l_sc, acc_sc, *, lam_init):
    tq = q_ref.shape[0]
    n_chunks = vt_ref.shape[0]
    q = q_ref[...]
    lane = lax.broadcasted_iota(jnp.int32, q.shape, 1)
    zero = jnp.zeros_like(q)
    qz = jnp.concatenate([jnp.where(lane < HEAD_DIM, q, zero), jnp.where(lane >= HEAD_DIM, q, zero)], axis=0)

    def scores(c):
        ks = pl.multiple_of(c * DIFF_TK, DIFF_TK)
        k = k_ref[pl.ds(ks, DIFF_TK), :]
        return lax.dot_general(k, qz, _NT, preferred_element_type=F32)

    @pl.when(bounded_ref[0] != 0)
    def _():
        l = jnp.zeros((1, 2 * tq), F32)
        acc = jnp.zeros((vt_ref.shape[1], 2 * tq), F32)
        for c in range(n_chunks):
            p = jnp.exp2(scores(c))
            l = l + jnp.sum(p, axis=0, keepdims=True)
            acc = acc + jnp.dot(vt_ref[c], p.astype(BF16), preferred_element_type=F32)
        l_sc[...] = l
        acc_sc[...] = acc

    @pl.when(bounded_ref[0] == 0)
    def _():
        def body(c, carry):
            m, l, acc = carry
            st = scores(c)
            m_new = jnp.maximum(m, jnp.max(st, axis=0, keepdims=True))
            alpha = jnp.exp2(m - m_new)
            p = jnp.exp2(st - m_new)
            l = alpha * l + jnp.sum(p, axis=0, keepdims=True)
            acc = alpha * acc + jnp.dot(vt_ref[c], p.astype(BF16), preferred_element_type=F32)
            return m_new, l, acc

        init = (jnp.full((1, 2 * tq), -jnp.inf, F32), jnp.zeros((1, 2 * tq), F32),
                jnp.zeros((vt_ref.shape[1], 2 * tq), F32))
        _, l, acc = lax.fori_loop(0, n_chunks, body, init)
        l_sc[...] = l
        acc_sc[...] = acc

    o = acc_sc[...] / l_sc[...]
    lam = (jnp.exp(jnp.sum(lq1_ref[...] * lk1_ref[...], axis=-1, keepdims=True))
           - jnp.exp(jnp.sum(lq2_ref[...] * lk2_ref[...], axis=-1, keepdims=True)) + lam_init)
    obt = o[:, :tq] - lam * o[:, tq:]
    ms = jnp.mean(obt * obt, axis=0, keepdims=True)
    obn = obt * lax.rsqrt(ms + EPS)
    o_ref[...] = (obn.T * (sg_ref[...] * (1.0 - lam_init))).astype(BF16)


def _diff_attn(qk, vt, score_bounded, lam_vecs, subln_g, bsz, seq, lam_init):
    t = qk.shape[0]
    tq = min(512, seq)
    qpt = seq // tq
    cps = seq // DIFF_TK
    hw = 2 * HEAD_DIM
    kern = functools.partial(_diff_attn_kernel, lam_init=lam_init)
    vec_spec = pl.BlockSpec((1, HEAD_DIM), lambda b, h, i: (0, 0))
    return pl.pallas_call(
        kern,
        out_shape=jax.ShapeDtypeStruct((t, DIFF_HEADS * hw), BF16),
        grid=(bsz, DIFF_HEADS, qpt),
        in_specs=[pl.BlockSpec(memory_space=pltpu.SMEM),
                  pl.BlockSpec((tq, hw), lambda b, h, i: (b * qpt + i, h)),
                  pl.BlockSpec((seq, hw), lambda b, h, i: (b, DIFF_HEADS + h)),
                  pl.BlockSpec((cps, hw, DIFF_TK), lambda b, h, i: (b, h, 0)),
                  vec_spec, vec_spec, vec_spec, vec_spec,
                  pl.BlockSpec((1, hw), lambda b, h, i: (0, 0))],
        out_specs=pl.BlockSpec((tq, hw), lambda b, h, i: (b * qpt + i, h)),
        scratch_shapes=[pltpu.VMEM((1, 2 * tq), F32), pltpu.VMEM((hw, 2 * tq), F32)],
        compiler_params=_cparams(3),
        name="diff_attn",
    )(score_bounded, qk, qk, vt, *lam_vecs, subln_g)


def _merge_kernel(o0_ref, l0_ref, o1_ref, l1_ref, o2_ref, l2_ref, ob_ref, ga_ref, gb_ref, x_ref, mod_ref,
                  ng_ref, wpa_ref, wpb_ref, wo_ref, x1_ref, h2_ref, nat_sc):
    oa = _combine_dilated_groups((o0_ref, o1_ref, o2_ref), (l0_ref, l1_ref, l2_ref), nat_sc, x_ref.shape[0])
    ba = jnp.dot(oa.astype(BF16), wpa_ref[...], preferred_element_type=F32)
    bb = jnp.dot(ob_ref[...], wpb_ref[...], preferred_element_type=F32)
    mixed = ga_ref[...].astype(F32) * ba + gb_ref[...].astype(F32) * bb
    mo = jnp.dot(mixed.astype(BF16), wo_ref[...], preferred_element_type=F32)
    m = mod_ref[0]
    x1 = x_ref[...] + m[2:3] * mo
    x1_ref[...] = x1
    h2_ref[...] = _modulated_norm(x1, ng_ref[...], m[4:5], m[3:4]).astype(BF16)


def _merge(dil_outs, dil_lses, ob, gates, x2, mod3, norm2_g, wpa, wpb, wo, bsz, seq):
    t, d = x2.shape
    tm = min(512, seq)
    spt = seq // tm
    full = lambda i: (0, 0)
    dil_specs, dil_args = [], []
    for (o, l), (_, dil) in zip(zip(dil_outs, dil_lses), DIL_GROUPS):
        shape4 = (bsz, dil, seq // dil, DIL_GROUP_WIDTH)
        spec = pl.BlockSpec((1, dil, tm // dil, DIL_GROUP_WIDTH), lambda i: (i // spt, 0, i % spt, 0))
        dil_specs += [spec, spec]
        dil_args += [o.reshape(shape4), l.reshape(shape4)]
    return pl.pallas_call(
        _merge_kernel,
        out_shape=(jax.ShapeDtypeStruct((t, d), F32), jax.ShapeDtypeStruct((t, d), BF16)),
        grid=(t // tm,),
        in_specs=dil_specs + [
                  pl.BlockSpec((tm, d), lambda i: (i, 0)),
                  pl.BlockSpec((tm, d), lambda i: (i, 0)),
                  pl.BlockSpec((tm, d), lambda i: (i, 1)),
                  pl.BlockSpec((tm, d), lambda i: (i, 0)),
                  pl.BlockSpec((1, 6, d), lambda i: (i // spt, 0, 0)),
                  pl.BlockSpec((1, d), full),
                  pl.BlockSpec(wpa.shape, full),
                  pl.BlockSpec(wpb.shape, full),
                  pl.BlockSpec(wo.shape, full)],
        out_specs=(pl.BlockSpec((tm, d), lambda i: (i, 0)), pl.BlockSpec((tm, d), lambda i: (i, 0))),
        scratch_shapes=[pltpu.VMEM((DIL_GROUP_WIDTH // LANES, tm, LANES), F32)],
        compiler_params=_cparams(1),
        name="merge_outproj_norm2",
    )(*dil_args, ob, gates, gates, x2, mod3, norm2_g, wpa, wpb, wo)


def _extract_topk(s, idx_f, val_sc, idx_sc):
    for k in range(PEER_TOPK):
        m = jnp.max(s, axis=0, keepdims=True)
        idx = jnp.min(jnp.where(s == m, idx_f, 1e9), axis=0, keepdims=True)
        val_sc[k:k + 1, :] = m
        idx_sc[k:k + 1, :] = idx
        s = jnp.where(idx_f == idx, -jnp.inf, s)


def _peer_topk_kernel(h_ref, wq_ref, sk_ref, e1_ref, e2_ref, g_ref,
                      q_sc, v1_sc, i1_sc, v2_sc, i2_sc, top_sc, lab_sc, e1t_sc, e2t_sc, gt_sc):
    tm = h_ref.shape[0]
    q = jnp.dot(h_ref[...], wq_ref[...], preferred_element_type=F32).astype(BF16)
    for piece in range(2 * PEER_HEADS):
        q_sc[piece] = q[:, piece * PEER_HALF:(piece + 1) * PEER_HALF]
    key_idx = lax.broadcasted_iota(jnp.int32, (PEER_N_KEYS, tm), 0).astype(F32)
    sub8 = lax.broadcasted_iota(jnp.int32, (8, tm), 0).astype(F32)
    neg = jnp.full((8, tm), -jnp.inf, F32)

    def head(h, carry):
        for side, (v_sc, i_sc) in enumerate(((v1_sc, i1_sc), (v2_sc, i2_sc))):
            qs = q_sc[2 * h + side]
            st = lax.dot_general(sk_ref[2 * h + side], qs, _NT, preferred_element_type=F32)
            _extract_topk(st, key_idx, v_sc, i_sc)
        v2lo = v2_sc[0:8, :]
        cands = [v1_sc[0:1, :] + v2lo, v1_sc[0:1, :] + v2_sc[8:16, :]]
        labels = [sub8, sub8 + 8.0]
        for a in range(1, 8):
            nb = PEER_TOPK // (a + 1)
            cands.append(jnp.where(sub8 < nb, v1_sc[a:a + 1, :] + v2lo, neg))
            labels.append(sub8 + float(a * PEER_TOPK))
        cands.append(v1_sc[8:16, :] + v2_sc[0:1, :])
        labels.append((sub8 + 8.0) * float(PEER_TOPK))
        cand = jnp.concatenate(cands, axis=0)
        label = jnp.concatenate(labels, axis=0)
        _extract_topk(cand, label, top_sc, lab_sc)
        top = top_sc[...]
        lab = lab_sc[...]
        a_sel = jnp.floor(lab * (1.0 / PEER_TOPK))
        b_sel = lab - a_sel * PEER_TOPK
        e1 = jnp.zeros_like(top)
        e2 = jnp.zeros_like(top)
        for r in range(PEER_TOPK):
            e1 = jnp.where(a_sel == float(r), i1_sc[r:r + 1, :], e1)
            e2 = jnp.where(b_sel == float(r), i2_sc[r:r + 1, :], e2)
        p = jnp.exp(top - jnp.max(top, axis=0, keepdims=True))
        gate = p / jnp.sum(p, axis=0, keepdims=True)
        row = pl.multiple_of(h * PEER_TOPK, PEER_TOPK)
        e1t_sc[pl.ds(row, PEER_TOPK), :] = e1
        e2t_sc[pl.ds(row, PEER_TOPK), :] = e2
        gt_sc[pl.ds(row, PEER_TOPK), :] = gate
        return carry

    lax.fori_loop(0, PEER_HEADS, head, 0)
    e1_ref[...] = e1t_sc[...].T
    e2_ref[...] = e2t_sc[...].T
    g_ref[...] = gt_sc[...].T


def _peer_topk(h2, wq, sk):
    t, d = h2.shape
    tm = min(512, t)
    nsel = PEER_HEADS * PEER_TOPK
    out = jax.ShapeDtypeStruct((t, nsel), F32)
    spec = pl.BlockSpec((tm, nsel), lambda i: (i, 0))
    small = lambda: pltpu.VMEM((PEER_TOPK, tm), F32)
    big = lambda: pltpu.VMEM((nsel, tm), F32)
    return pl.pallas_call(
        _peer_topk_kernel,
        out_shape=(out, out, out),
        grid=(t // tm,),
        in_specs=[pl.BlockSpec((tm, d), lambda i: (i, 0)),
                  pl.BlockSpec(wq.shape, lambda i: (0, 0)),
                  pl.BlockSpec(sk.shape, lambda i: (0, 0, 0))],
        out_specs=(spec, spec, spec),
        scratch_shapes=[pltpu.VMEM((2 * PEER_HEADS, tm, PEER_HALF), BF16),
                        small(), small(), small(), small(), small(), small(), big(), big(), big()],
        compiler_params=_cparams(1),
        name="peer_topk",
    )(h2, wq, sk)


def _peer_gates_kernel(e1_ref, e2_ref, g_ref, o_ref, g_sc, *, pitch):
    tm = e1_ref.shape[0]
    row_idx = lax.broadcasted_iota(jnp.int32, (PEER_N_KEYS, e1_ref.shape[1]), 0).astype(F32)

    def token(t, carry):
        r1 = e1_ref[pl.ds(t, 1), :]
        r2 = e2_ref[pl.ds(t, 1), :]
        gt = g_ref[pl.ds(t, 1), :]
        sel_i = jnp.where(row_idx == r1, 1.0, 0.0).astype(BF16)
        sel_j = jnp.where(row_idx == r2, gt, 0.0).astype(BF16)
        gm = lax.dot_general(sel_i, sel_j, _NT, preferred_element_type=F32)
        g_sc[pl.ds(t, PEER_N_KEYS, stride=pitch), :] = gm
        return carry

    lax.fori_loop(0, tm, token, 0, unroll=32)

    def emit(i, carry):
        start = pl.multiple_of(i * pitch, 8)
        o_ref[i] = g_sc[pl.ds(start, tm), :].astype(BF16)
        return carry

    lax.fori_loop(0, PEER_N_KEYS, emit, 0)


def _peer_gates(e1, e2, gate):
    t, nsel = e1.shape
    tm = min(256, t)
    pitch = tm + 8
    spec = pl.BlockSpec((tm, nsel), lambda i: (i, 0))
    return pl.pallas_call(
        functools.partial(_peer_gates_kernel, pitch=pitch),
        out_shape=jax.ShapeDtypeStruct((PEER_N_KEYS, t, PEER_N_KEYS), BF16),
        grid=(t // tm,),
        in_specs=[spec, spec, spec],
        out_specs=pl.BlockSpec((PEER_N_KEYS, tm, PEER_N_KEYS), lambda i: (0, i, 0)),
        scratch_shapes=[pltpu.VMEM((PEER_N_KEYS * pitch, PEER_N_KEYS), F32)],
        compiler_params=_cparams(1),
        name="peer_gate_matrix",
    )(e1, e2, gate)


def _peer_dense_kernel(h_ref, u_ref, v_ref, g_ref, x1_ref, mod_ref, o_ref, acc_sc):
    c = pl.program_id(1)

    @pl.when(c == 0)
    def _():
        acc_sc[...] = jnp.zeros_like(acc_sc)

    a = lax.dot_general(h_ref[...], u_ref[...], _NT, preferred_element_type=F32)
    act = 0.5 * a * (1.0 + lax.erf(a * (1.0 / math.sqrt(2.0))))
    ws = [(act[:, ib * LANES:(ib + 1) * LANES] * g_ref[ib].astype(F32)).astype(BF16)
          for ib in range(g_ref.shape[0])]
    w = jnp.concatenate(ws, axis=1)
    acc_sc[...] += jnp.dot(w, v_ref[...], preferred_element_type=F32)

    @pl.when(c == pl.num_programs(1) - 1)
    def _():
        o_ref[...] = x1_ref[...] + mod_ref[0][5:6] * acc_sc[...]


def _peer_dense(h2, u, v, gmat, x1, mod3, seq):
    t, d = h2.shape
    n_exp = u.shape[0]
    tm = min(1024, seq)
    spt = seq // tm
    tn = 1024
    return pl.pallas_call(
        _peer_dense_kernel,
        out_shape=jax.ShapeDtypeStruct((t, d), F32),
        grid=(t // tm, n_exp // tn),
        in_specs=[pl.BlockSpec((tm, d), lambda i, c: (i, 0)),
                  pl.BlockSpec((tn, d), lambda i, c: (c, 0)),
                  pl.BlockSpec((tn, d), lambda i, c: (c, 0)),
                  pl.BlockSpec((tn // PEER_N_KEYS, tm, PEER_N_KEYS), lambda i, c: (c, i, 0)),
                  pl.BlockSpec((tm, d), lambda i, c: (i, 0)),
                  pl.BlockSpec((1, 6, d), lambda i, c: (i // spt, 0, 0))],
        out_specs=pl.BlockSpec((tm, d), lambda i, c: (i, 0)),
        scratch_shapes=[pltpu.VMEM((tm, d), F32)],
        compiler_params=_cparams(2),
        name="peer_dense",
    )(h2, u, v, gmat, x1, mod3)


def _layer(x, c, cos, sin, lam_init, w_ada, b_ada, norm1_g, w_in, b_gate, qn_a, kn_a, w_proj_a,
           qn_b, kn_b, lam_q1, lam_k1, lam_q2, lam_k2, subln_g, w_proj_b, w_out, norm2_g,
           w_query, sub_keys, expert_u, expert_v):
    bsz, seq, d = x.shape
    t = bsz * seq
    x2 = x.reshape(t, d)
    mod3 = _ada(c, w_ada, b_ada).reshape(bsz, 6, d)
    gmat = _group_sum_matrix()
    scale = HEAD_DIM ** -0.5

    dil_w = len(DIL_GROUPS) * DIL_GROUP_WIDTH
    diff_w = DIFF_HEADS * 2 * HEAD_DIM
    a_cols = 3 * dil_w
    w_in_bf = w_in.astype(BF16)
    ones = lambda n: jnp.ones((n,), F32)
    gain_a = jnp.concatenate([jnp.tile(qn_a, dil_w // HEAD_DIM) * scale, jnp.tile(kn_a, dil_w // HEAD_DIM),
                              ones(dil_w)]).reshape(1, a_cols)
    qkv_groups = _inproj_a(x2, mod3, norm1_g.reshape(1, d), w_in_bf[:, :a_cols], gain_a, cos, sin, gmat, bsz, seq)

    n_b = w_in.shape[1] - a_cols
    gain_b = jnp.concatenate([jnp.tile(qn_b, diff_w // HEAD_DIM) * (scale * LOG2E),
                              jnp.tile(kn_b, diff_w // HEAD_DIM), ones(n_b - 2 * diff_w)]).reshape(1, n_b)
    bias_b = jnp.concatenate([jnp.zeros((3 * diff_w,), F32), b_gate]).reshape(1, n_b)
    qk_b, vt_b, gates = _inproj_b(x2, mod3, norm1_g.reshape(1, d), w_in_bf[:, a_cols:], gain_b, bias_b,
                                  cos, sin, gmat, seq, diff_w, diff_w)

    outs, lses = [], []
    for qkv in qkv_groups:
        o, l = _dilated_attn(qkv)
        outs.append(o)
        lses.append(l)

    lam_vecs = [v.reshape(1, HEAD_DIM) for v in (lam_q1, lam_k1, lam_q2, lam_k2)]
    score_cap = (HEAD_DIM * scale * LOG2E * 1.01) * jnp.max(jnp.abs(qn_b)) * jnp.max(jnp.abs(kn_b))
    score_bounded = (score_cap <= DIFF_SCORE_BOUND).astype(jnp.int32).reshape(1)
    ob = _diff_attn(qk_b, vt_b, score_bounded, lam_vecs, subln_g.reshape(1, 2 * HEAD_DIM), bsz, seq, lam_init)

    x1, h2 = _merge(outs, lses, ob, gates, x2, mod3, norm2_g.reshape(1, d), w_proj_a.astype(BF16),
                    w_proj_b.astype(BF16), w_out.astype(BF16), bsz, seq)

    sk = sub_keys.astype(BF16).reshape(PEER_HEADS * 2, PEER_N_KEYS, PEER_HALF)
    e1, e2, gate = _peer_topk(h2, w_query.astype(BF16), sk)
    gdense = _peer_gates(e1, e2, gate)
    out = _peer_dense(h2, expert_u.astype(BF16), expert_v.astype(BF16), gdense, x1, mod3, seq)
    return out.reshape(bsz, seq, d)


def kernel(x, c, positions, w_ada, b_ada, norm1_g, w_in, b_gate, qn_a, kn_a, w_proj_a, qn_b, kn_b,
           lam_q1, lam_k1, lam_q2, lam_k2, subln_g, w_proj_b, w_out, norm2_g, w_query, sub_keys,
           expert_u, expert_v):
    cos, sin = _rope_tables(positions)
    for l in range(w_ada.shape[0]):
        lam_init = 0.8 - 0.6 * math.exp(-0.3 * l)
        x = _layer(x, c, cos, sin, lam_init, w_ada[l], b_ada[l], norm1_g[l], w_in[l], b_gate[l],
                   qn_a[l], kn_a[l], w_proj_a[l], qn_b[l], kn_b[l], lam_q1[l], lam_k1[l], lam_q2[l],
                   lam_k2[l], subln_g[l], w_proj_b[l], w_out[l], norm2_g[l], w_query[l], sub_keys[l],
                   expert_u[l], expert_v[l])
    return x
```

```python
import functools
import math

import jax
import jax.numpy as jnp
from jax import lax
from jax.experimental import pallas as pl
from jax.experimental.pallas import tpu as pltpu

F32 = jnp.float32
BF16 = jnp.bfloat16

HEAD_DIM = 64
ROPE_THETA = 10000.0
EPS = 1e-6
NEG_INF = -1e30

DIL_GROUPS = ((128, 1), (512, 4), (2048, 16))
DIL_HEADS_PER_GROUP = 4
DIL_GROUP_WIDTH = DIL_HEADS_PER_GROUP * HEAD_DIM
DIL_SIDE = 64

DIFF_HEADS = 8
DIFF_TK = 1024
LOG2E = 1.4426950408889634
DIFF_SCORE_BOUND = 60.0
PEER_HEADS = 8
PEER_N_KEYS = 128
PEER_TOPK = 16
PEER_HALF = 128

LANES = 128
VMEM_LIMIT = 56 * 1024 * 1024

_NT = (((1,), (1,)), ((), ()))


def _cparams(n_axes):
    return pltpu.CompilerParams(dimension_semantics=("arbitrary",) * n_axes,
                                vmem_limit_bytes=VMEM_LIMIT)


def _ada_kernel(c_ref, w_ref, b_ref, o_ref):
    c = c_ref[...]
    sc = (c * jax.nn.sigmoid(c)).astype(BF16)
    o_ref[...] = jnp.dot(sc, w_ref[...].astype(BF16), preferred_element_type=F32) + b_ref[...]


def _ada(c, w_ada, b_ada):
    bsz, d = c.shape
    n = w_ada.shape[1]
    tn = 1024
    return pl.pallas_call(
        _ada_kernel,
        out_shape=jax.ShapeDtypeStruct((bsz, n), F32),
        grid=(n // tn,),
        in_specs=[pl.BlockSpec((bsz, d), lambda j: (0, 0)),
                  pl.BlockSpec((d, tn), lambda j: (0, j)),
                  pl.BlockSpec((1, tn), lambda j: (0, j))],
        out_specs=pl.BlockSpec((bsz, tn), lambda j: (0, j)),
        compiler_params=_cparams(1),
        name="ada_mod",
    )(c, w_ada, b_ada.reshape(1, n))


def _rope_kernel(pos_ref, freq_ref, cos_ref, sin_ref):
    ang = pos_ref[...] * freq_ref[...]
    lane = lax.broadcasted_iota(jnp.int32, ang.shape, 1)
    first_half = (lane % HEAD_DIM) < (HEAD_DIM // 2)
    cos_ref[...] = jnp.cos(ang)
    s = jnp.sin(ang)
    sin_ref[...] = jnp.where(first_half, -s, s)


def _rope_tables(positions):
    t = positions.size
    tm = min(1024, t)
    pos = positions.reshape(t, 1).astype(F32)
    inv_freq = 1.0 / (ROPE_THETA ** (jnp.arange(0, HEAD_DIM, 2, dtype=F32) / HEAD_DIM))
    freq = jnp.tile(inv_freq, LANES // (HEAD_DIM // 2)).reshape(1, LANES)
    return pl.pallas_call(
        _rope_kernel,
        out_shape=(jax.ShapeDtypeStruct((t, LANES), F32), jax.ShapeDtypeStruct((t, LANES), F32)),
        grid=(t // tm,),
        in_specs=[pl.BlockSpec((tm, 1), lambda i: (i, 0)),
                  pl.BlockSpec((1, LANES), lambda i: (0, 0))],
        out_specs=(pl.BlockSpec((tm, LANES), lambda i: (i, 0)),
                   pl.BlockSpec((tm, LANES), lambda i: (i, 0))),
        compiler_params=_cparams(1),
        name="rope_tables",
    )(pos, freq)


def _modulated_norm(x, g, scale, shift):
    ms = jnp.mean(x * x, axis=-1, keepdims=True)
    return (x * lax.rsqrt(ms + EPS)) * g * (1.0 + scale) + shift


def _head_norm_rope(acc, gain, cos, sin, gmat):
    ss = jnp.dot((acc * acc).astype(BF16), gmat, preferred_element_type=F32)
    yn = acc * lax.rsqrt(ss * (1.0 / HEAD_DIM) + EPS) * gain
    lane = lax.broadcasted_iota(jnp.int32, cos.shape, 1)
    first_half = (lane % HEAD_DIM) < (HEAD_DIM // 2)
    outs = []
    for hf in range(acc.shape[1] // LANES):
        y = yn[:, hf * LANES:(hf + 1) * LANES]
        up = pltpu.roll(y, HEAD_DIM // 2, axis=1)
        down = pltpu.roll(y, LANES - HEAD_DIM // 2, axis=1)
        swapped = jnp.where(first_half, down, up)
        outs.append(y * cos + swapped * sin)
    return jnp.concatenate(outs, axis=1)


def _group_sum_matrix():
    r = lax.broadcasted_iota(jnp.int32, (2 * LANES, 2 * LANES), 0) // HEAD_DIM
    c = lax.broadcasted_iota(jnp.int32, (2 * LANES, 2 * LANES), 1) // HEAD_DIM
    return (r == c).astype(BF16)


def _inproj_a_kernel(x_ref, mod_ref, ng_ref, w_ref, gain_ref, cos_ref, sin_ref, gmat_ref,
                     o0_ref, o1_ref, o2_ref, h_sc, y_sc):
    j = pl.program_id(1)
    tm = x_ref.shape[0]

    @pl.when(j == 0)
    def _():
        m = mod_ref[0]
        h_sc[...] = _modulated_norm(x_ref[...], ng_ref[...], m[1:2], m[0:1]).astype(BF16)

    acc = jnp.dot(h_sc[...], w_ref[...], preferred_element_type=F32)
    outs = (o0_ref, o1_ref, o2_ref)

    def emit(get_piece):
        for g, (_, dil) in enumerate(DIL_GROUPS):
            y = get_piece(g)
            if dil == 1:
                outs[g][0, 0, 0] = y.astype(BF16)
            else:
                rows = tm // dil
                for hf in range(DIL_GROUP_WIDTH // LANES):
                    y_sc[hf] = y[:, hf * LANES:(hf + 1) * LANES]
                for r in range(dil):
                    piece = jnp.concatenate(
                        [y_sc[hf, pl.ds(r, rows, stride=dil), :] for hf in range(DIL_GROUP_WIDTH // LANES)],
                        axis=1)
                    outs[g][0, 0, r] = piece.astype(BF16)

    @pl.when(j < 2)
    def _():
        cos = cos_ref[...]
        sin = sin_ref[...]
        gmat = gmat_ref[...]
        emit(lambda g: _head_norm_rope(acc[:, g * DIL_GROUP_WIDTH:(g + 1) * DIL_GROUP_WIDTH],
                                       gain_ref[:, g * DIL_GROUP_WIDTH:(g + 1) * DIL_GROUP_WIDTH],
                                       cos, sin, gmat))

    @pl.when(j == 2)
    def _():
        emit(lambda g: acc[:, g * DIL_GROUP_WIDTH:(g + 1) * DIL_GROUP_WIDTH])


def _inproj_a(x2, mod3, norm_g, w_a, gain_a, cos, sin, gmat, bsz, seq):
    t, d = x2.shape
    tm = min(1024, seq)
    spt = seq // tm
    n = w_a.shape[1]
    tn = n // 3
    out_shapes, out_specs = [], []
    for _, dil in DIL_GROUPS:
        out_shapes.append(jax.ShapeDtypeStruct((3, bsz, dil, seq // dil, DIL_GROUP_WIDTH), BF16))
        out_specs.append(pl.BlockSpec((1, 1, dil, tm // dil, DIL_GROUP_WIDTH),
                                      lambda i, j: (j, i // spt, 0, i % spt, 0)))
    return pl.pallas_call(
        _inproj_a_kernel,
        out_shape=tuple(out_shapes),
        grid=(t // tm, 3),
        in_specs=[pl.BlockSpec((tm, d), lambda i, j: (i, 0)),
                  pl.BlockSpec((1, 6, d), lambda i, j: (i // spt, 0, 0)),
                  pl.BlockSpec((1, d), lambda i, j: (0, 0)),
                  pl.BlockSpec((d, tn), lambda i, j: (0, j)),
                  pl.BlockSpec((1, tn), lambda i, j: (0, j)),
                  pl.BlockSpec((tm, LANES), lambda i, j: (i, 0)),
                  pl.BlockSpec((tm, LANES), lambda i, j: (i, 0)),
                  pl.BlockSpec((2 * LANES, 2 * LANES), lambda i, j: (0, 0))],
        out_specs=tuple(out_specs),
        scratch_shapes=[pltpu.VMEM((tm, d), BF16),
                        pltpu.VMEM((DIL_GROUP_WIDTH // LANES, tm, LANES), F32)],
        compiler_params=_cparams(2),
        name="inproj_dilated",
    )(x2, mod3, norm_g, w_a, gain_a, cos, sin, gmat)


def _inproj_b_kernel(x_ref, mod_ref, ng_ref, w_ref, gain_ref, bias_ref, cos_ref, sin_ref, gmat_ref,
                     qk_ref, vt_ref, gate_ref, h_sc, *, n_rope_tiles, n_plain_tiles):
    j = pl.program_id(1)

    @pl.when(j == 0)
    def _():
        m = mod_ref[0]
        h_sc[...] = _modulated_norm(x_ref[...], ng_ref[...], m[1:2], m[0:1]).astype(BF16)

    acc = jnp.dot(h_sc[...], w_ref[...], preferred_element_type=F32)
    tn = acc.shape[1]

    @pl.when(j < n_rope_tiles)
    def _():
        cos = cos_ref[...]
        sin = sin_ref[...]
        gmat = gmat_ref[...]
        for p in range(tn // (2 * LANES)):
            sl = slice(p * 2 * LANES, (p + 1) * 2 * LANES)
            qk_ref[:, sl] = _head_norm_rope(acc[:, sl], gain_ref[:, sl], cos, sin, gmat).astype(BF16)

    @pl.when((j >= n_rope_tiles) & (j < n_rope_tiles + n_plain_tiles))
    def _():
        for cc in range(vt_ref.shape[0]):
            vt_ref[cc] = acc[cc * DIFF_TK:(cc + 1) * DIFF_TK, :].T.astype(BF16)

    @pl.when(j >= n_rope_tiles + n_plain_tiles)
    def _():
        gate_ref[...] = jax.nn.sigmoid(acc + bias_ref[...]).astype(BF16)


def _inproj_b(x2, mod3, norm_g, w_b, gain_b, bias_b, cos, sin, gmat, seq, qk_width, v_width):
    t, d = x2.shape
    tm = min(1024, seq)
    spt = seq // tm
    n = w_b.shape[1]
    tn = 1024
    n_rope, n_plain = 2 * qk_width // tn, v_width // tn
    n_gate = n // tn - n_rope - n_plain
    kern = functools.partial(_inproj_b_kernel, n_rope_tiles=n_rope, n_plain_tiles=n_plain)
    return pl.pallas_call(
        kern,
        out_shape=(jax.ShapeDtypeStruct((t, n_rope * tn), BF16),
                   jax.ShapeDtypeStruct((t // DIFF_TK, v_width, DIFF_TK), BF16),
                   jax.ShapeDtypeStruct((t, n_gate * tn), BF16)),
        grid=(t // tm, n // tn),
        in_specs=[pl.BlockSpec((tm, d), lambda i, j: (i, 0)),
                  pl.BlockSpec((1, 6, d), lambda i, j: (i // spt, 0, 0)),
                  pl.BlockSpec((1, d), lambda i, j: (0, 0)),
                  pl.BlockSpec((d, tn), lambda i, j: (0, j)),
                  pl.BlockSpec((1, tn), lambda i, j: (0, j)),
                  pl.BlockSpec((1, tn), lambda i, j: (0, j)),
                  pl.BlockSpec((tm, LANES), lambda i, j: (i, 0)),
                  pl.BlockSpec((tm, LANES), lambda i, j: (i, 0)),
                  pl.BlockSpec((2 * LANES, 2 * LANES), lambda i, j: (0, 0))],
        out_specs=(pl.BlockSpec((tm, tn), lambda i, j: (i, jnp.minimum(j, n_rope - 1))),
                   pl.BlockSpec((tm // DIFF_TK, tn, DIFF_TK),
                                lambda i, j: (i, jnp.clip(j - n_rope, 0, n_plain - 1), 0)),
                   pl.BlockSpec((tm, tn), lambda i, j: (i, jnp.clip(j - n_rope - n_plain, 0, n_gate - 1)))),
        scratch_shapes=[pltpu.VMEM((tm, d), BF16)],
        compiler_params=_cparams(2),
        name="inproj_diff_gates",
    )(x2, mod3, norm_g, w_b, gain_b, bias_b, cos, sin, gmat)


def _dilated_attn_kernel(q_ref, k_ref, v_ref, o_ref, lse_ref, *, tq, kw, length, n_sub):
    qi = pl.program_id(1)
    lane_head = lax.broadcasted_iota(jnp.int32, (tq, DIL_GROUP_WIDTH), 1) // HEAD_DIM
    for sub in range(n_sub):
        rows = slice(sub * tq, (sub + 1) * tq)
        q0 = (qi * n_sub + sub) * tq
        kstart = jnp.clip(q0 - DIL_SIDE, 0, length - kw)
        kstart = pl.multiple_of(kstart, DIL_SIDE)
        q = q_ref[0, 0, rows, :]
        k = k_ref[0, 0, pl.ds(kstart, kw), :]
        v = v_ref[0, 0, pl.ds(kstart, kw), :]
        n_h = DIL_HEADS_PER_GROUP
        qs = jnp.concatenate([jnp.where(lane_head == h, q, jnp.zeros_like(q)) for h in range(n_h)], axis=0)
        s = lax.dot_general(qs, k, _NT, preferred_element_type=F32)
        row = lax.broadcasted_iota(jnp.int32, (n_h * tq, kw), 0)
        qidx = q0 + row % tq
        kidx = kstart + lax.broadcasted_iota(jnp.int32, (n_h * tq, kw), 1)
        s = jnp.where(jnp.abs(kidx - qidx) <= DIL_SIDE, s, NEG_INF)
        m = jnp.max(s, axis=-1, keepdims=True)
        p = jnp.exp(s - m)
        l = jnp.sum(p, axis=-1, keepdims=True)
        pb = p.astype(BF16)
        lse = m + jnp.log(l)
        out = jnp.zeros((tq, DIL_GROUP_WIDTH), F32)
        lse_full = jnp.zeros((tq, DIL_GROUP_WIDTH), F32)
        for h in range(n_h):
            in_head = lane_head == h
            hr = slice(h * tq, (h + 1) * tq)
            oh = jnp.dot(pb[hr], v, preferred_element_type=F32)
            out = jnp.where(in_head, oh / l[hr], out)
            lse_full = jnp.where(in_head, lse[hr], lse_full)
        o_ref[0, rows, :] = out.astype(BF16)
        lse_ref[0, rows, :] = lse_full


def _dilated_attn(qkv):
    _, bsz, dil, length, w = qkv.shape
    bd = bsz * dil
    qkv = qkv.reshape(3, bd, length, w)
    tq = min(128, length)
    kw = min(tq + 2 * DIL_SIDE, length)
    n_sub = min(4, length // tq)
    tb = tq * n_sub
    kern = functools.partial(_dilated_attn_kernel, tq=tq, kw=kw, length=length, n_sub=n_sub)
    return pl.pallas_call(
        kern,
        out_shape=(jax.ShapeDtypeStruct((bd, length, w), BF16), jax.ShapeDtypeStruct((bd, length, w), F32)),
        grid=(bd, length // tb),
        in_specs=[pl.BlockSpec((1, 1, tb, w), lambda b, i: (0, b, i, 0)),
                  pl.BlockSpec((1, 1, length, w), lambda b, i: (1, b, 0, 0)),
                  pl.BlockSpec((1, 1, length, w), lambda b, i: (2, b, 0, 0))],
        out_specs=(pl.BlockSpec((1, tb, w), lambda b, i: (b, i, 0)),
                   pl.BlockSpec((1, tb, w), lambda b, i: (b, i, 0))),
        compiler_params=_cparams(2),
        name="dilated_attn",
    )(qkv, qkv, qkv)


def _combine_dilated_groups(o_refs, l_refs, nat_sc, tm):
    n_half = DIL_GROUP_WIDTH // LANES

    def natural(ref, dil):
        if dil == 1:
            return ref[0, 0].astype(F32)
        rows = tm // dil
        for r in range(dil):
            blk = ref[0, r].astype(F32)
            for hf in range(n_half):
                nat_sc[hf, pl.ds(r, rows, stride=dil), :] = blk[:, hf * LANES:(hf + 1) * LANES]
        return jnp.concatenate([nat_sc[hf] for hf in range(n_half)], axis=1)

    dils = [d for _, d in DIL_GROUPS]
    lses = [natural(r, d) for r, d in zip(l_refs, dils)]
    m = jnp.maximum(jnp.maximum(lses[0], lses[1]), lses[2])
    ws = [jnp.exp(l - m) for l in lses]
    den = ws[0] + ws[1] + ws[2]
    num = jnp.zeros_like(den)
    for w, r, d in zip(ws, o_refs, dils):
        num = num + w * natural(r, d)
    return num / den


def _diff_attn_kernel(bounded_ref, q_ref, k_ref, vt_ref, lq1_ref, lk1_ref, lq2_ref, lk2_ref, sg_ref, o_ref,
                      l_sc, acc_sc, *, lam_init):
    tq = q_ref.shape[0]
    n_chunks = vt_ref.shape[0]
    q = q_ref[...]
    lane = lax.broadcasted_iota(jnp.int32, q.shape, 1)
    zero = jnp.zeros_like(q)
    qz = jnp.concatenate([jnp.where(lane < HEAD_DIM, q, zero), jnp.where(lane >= HEAD_DIM, q, zero)], axis=0)

    def scores(c):
        ks = pl.multiple_of(c * DIFF_TK, DIFF_TK)
        k = k_ref[pl.ds(ks, DIFF_TK), :]
        return lax.dot_general(k, qz, _NT, preferred_element_type=F32)

    @pl.when(bounded_ref[0] != 0)
    def _():
        l = jnp.zeros((1, 2 * tq), F32)
        acc = jnp.zeros((vt_ref.shape[1], 2 * tq), F32)
        for c in range(n_chunks):
            p = jnp.exp2(scores(c))
            l = l + jnp.sum(p, axis=0, keepdims=True)
            acc = acc + jnp.dot(vt_ref[c], p.astype(BF16), preferred_element_type=F32)
        l_sc[...] = l
        acc_sc[...] = acc

    @pl.when(bounded_ref[0] == 0)
    def _():
        def body(c, carry):
            m, l, acc = carry
            st = scores(c)
            m_new = jnp.maximum(m, jnp.max(st, axis=0, keepdims=True))
            alpha = jnp.exp2(m - m_new)
            p = jnp.exp2(st - m_new)
            l = alpha * l + jnp.sum(p, axis=0, keepdims=True)
            acc = alpha * acc + jnp.dot(vt_ref[c], p.astype(BF16), preferred_element_type=F32)
            return m_new, l, acc

        init = (jnp.full((1, 2 * tq), -jnp.inf, F32), jnp.zeros((1, 2 * tq), F32),
                jnp.zeros((vt_ref.shape[1], 2 * tq), F32))
        _, l, acc = lax.fori_loop(0, n_chunks, body, init)
        l_sc[...] = l
        acc_sc[...] = acc

    o = acc_sc[...] / l_sc[...]
    lam = (jnp.exp(jnp.sum(lq1_ref[...] * lk1_ref[...], axis=-1, keepdims=True))
           - jnp.exp(jnp.sum(lq2_ref[...] * lk2_ref[...], axis=-1, keepdims=True)) + lam_init)
    obt = o[:, :tq] - lam * o[:, tq:]
    ms = jnp.mean(obt * obt, axis=0, keepdims=True)
    obn = obt * lax.rsqrt(ms + EPS)
    o_ref[...] = (obn.T * (sg_ref[...] * (1.0 - lam_init))).astype(BF16)


def _diff_attn(qk, vt, score_bounded, lam_vecs, subln_g, bsz, seq, lam_init):
    t = qk.shape[0]
    tq = min(1024, seq)
    qpt = seq // tq
    cps = seq // DIFF_TK
    hw = 2 * HEAD_DIM
    kern = functools.partial(_diff_attn_kernel, lam_init=lam_init)
    vec_spec = pl.BlockSpec((1, HEAD_DIM), lambda b, h, i: (0, 0))
    return pl.pallas_call(
        kern,
        out_shape=jax.ShapeDtypeStruct((t, DIFF_HEADS * hw), BF16),
        grid=(bsz, DIFF_HEADS, qpt),
        in_specs=[pl.BlockSpec(memory_space=pltpu.SMEM),
                  pl.BlockSpec((tq, hw), lambda b, h, i: (b * qpt + i, h)),
                  pl.BlockSpec((seq, hw), lambda b, h, i: (b, DIFF_HEADS + h)),
                  pl.BlockSpec((cps, hw, DIFF_TK), lambda b, h, i: (b, h, 0)),
                  vec_spec, vec_spec, vec_spec, vec_spec,
                  pl.BlockSpec((1, hw), lambda b, h, i: (0, 0))],
        out_specs=pl.BlockSpec((tq, hw), lambda b, h, i: (b * qpt + i, h)),
        scratch_shapes=[pltpu.VMEM((1, 2 * tq), F32), pltpu.VMEM((hw, 2 * tq), F32)],
        compiler_params=_cparams(3),
        name="diff_attn",
    )(score_bounded, qk, qk, vt, *lam_vecs, subln_g)


def _merge_kernel(o0_ref, l0_ref, o1_ref, l1_ref, o2_ref, l2_ref, ob_ref, ga_ref, gb_ref, x_ref, mod_ref,
                  ng_ref, wpa_ref, wpb_ref, wo_ref, x1_ref, h2_ref, nat_sc):
    oa = _combine_dilated_groups((o0_ref, o1_ref, o2_ref), (l0_ref, l1_ref, l2_ref), nat_sc, x_ref.shape[0])
    ba = jnp.dot(oa.astype(BF16), wpa_ref[...], preferred_element_type=F32)
    bb = jnp.dot(ob_ref[...], wpb_ref[...], preferred_element_type=F32)
    mixed = ga_ref[...].astype(F32) * ba + gb_ref[...].astype(F32) * bb
    mo = jnp.dot(mixed.astype(BF16), wo_ref[...], preferred_element_type=F32)
    m = mod_ref[0]
    x1 = x_ref[...] + m[2:3] * mo
    x1_ref[...] = x1
    h2_ref[...] = _modulated_norm(x1, ng_ref[...], m[4:5], m[3:4]).astype(BF16)


def _merge(dil_outs, dil_lses, ob, gates, x2, mod3, norm2_g, wpa, wpb, wo, bsz, seq):
    t, d = x2.shape
    tm = min(512, seq)
    spt = seq // tm
    full = lambda i: (0, 0)
    dil_specs, dil_args = [], []
    for (o, l), (_, dil) in zip(zip(dil_outs, dil_lses), DIL_GROUPS):
        shape4 = (bsz, dil, seq // dil, DIL_GROUP_WIDTH)
        spec = pl.BlockSpec((1, dil, tm // dil, DIL_GROUP_WIDTH), lambda i: (i // spt, 0, i % spt, 0))
        dil_specs += [spec, spec]
        dil_args += [o.reshape(shape4), l.reshape(shape4)]
    return pl.pallas_call(
        _merge_kernel,
        out_shape=(jax.ShapeDtypeStruct((t, d), F32), jax.ShapeDtypeStruct((t, d), BF16)),
        grid=(t // tm,),
        in_specs=dil_specs + [
                  pl.BlockSpec((tm, d), lambda i: (i, 0)),
                  pl.BlockSpec((tm, d), lambda i: (i, 0)),
                  pl.BlockSpec((tm, d), lambda i: (i, 1)),
                  pl.BlockSpec((tm, d), lambda i: (i, 0)),
                  pl.BlockSpec((1, 6, d), lambda i: (i // spt, 0, 0)),
                  pl.BlockSpec((1, d), full),
                  pl.BlockSpec(wpa.shape, full),
                  pl.BlockSpec(wpb.shape, full),
                  pl.BlockSpec(wo.shape, full)],
        out_specs=(pl.BlockSpec((tm, d), lambda i: (i, 0)), pl.BlockSpec((tm, d), lambda i: (i, 0))),
        scratch_shapes=[pltpu.VMEM((DIL_GROUP_WIDTH // LANES, tm, LANES), F32)],
        compiler_params=_cparams(1),
        name="merge_outproj_norm2",
    )(*dil_args, ob, gates, gates, x2, mod3, norm2_g, wpa, wpb, wo)


def _extract_topk(s, idx_f, val_sc, idx_sc):
    for k in range(PEER_TOPK):
        m = jnp.max(s, axis=0, keepdims=True)
        idx = jnp.min(jnp.where(s == m, idx_f, 1e9), axis=0, keepdims=True)
        val_sc[k:k + 1, :] = m
        idx_sc[k:k + 1, :] = idx
        s = jnp.where(idx_f == idx, -jnp.inf, s)


def _peer_topk_kernel(h_ref, wq_ref, sk_ref, e1_ref, e2_ref, g_ref,
                      q_sc, v1_sc, i1_sc, v2_sc, i2_sc, top_sc, lab_sc, e1t_sc, e2t_sc, gt_sc):
    tm = h_ref.shape[0]
    q = jnp.dot(h_ref[...], wq_ref[...], preferred_element_type=F32).astype(BF16)
    for piece in range(2 * PEER_HEADS):
        q_sc[piece] = q[:, piece * PEER_HALF:(piece + 1) * PEER_HALF]
    key_idx = lax.broadcasted_iota(jnp.int32, (PEER_N_KEYS, tm), 0).astype(F32)
    sub8 = lax.broadcasted_iota(jnp.int32, (8, tm), 0).astype(F32)
    neg = jnp.full((8, tm), -jnp.inf, F32)

    def head(h, carry):
        for side, (v_sc, i_sc) in enumerate(((v1_sc, i1_sc), (v2_sc, i2_sc))):
            qs = q_sc[2 * h + side]
            st = lax.dot_general(sk_ref[2 * h + side], qs, _NT, preferred_element_type=F32)
            _extract_topk(st, key_idx, v_sc, i_sc)
        v2lo = v2_sc[0:8, :]
        cands = [v1_sc[0:1, :] + v2lo, v1_sc[0:1, :] + v2_sc[8:16, :]]
        labels = [sub8, sub8 + 8.0]
        for a in range(1, 8):
            nb = PEER_TOPK // (a + 1)
            cands.append(jnp.where(sub8 < nb, v1_sc[a:a + 1, :] + v2lo, neg))
            labels.append(sub8 + float(a * PEER_TOPK))
        cands.append(v1_sc[8:16, :] + v2_sc[0:1, :])
        labels.append((sub8 + 8.0) * float(PEER_TOPK))
        cand = jnp.concatenate(cands, axis=0)
        label = jnp.concatenate(labels, axis=0)
        _extract_topk(cand, label, top_sc, lab_sc)
        top = top_sc[...]
        lab = lab_sc[...]
        a_sel = jnp.floor(lab * (1.0 / PEER_TOPK))
        b_sel = lab - a_sel * PEER_TOPK
        e1 = jnp.zeros_like(top)
        e2 = jnp.zeros_like(top)
        for r in range(PEER_TOPK):
            e1 = jnp.where(a_sel == float(r), i1_sc[r:r + 1, :], e1)
            e2 = jnp.where(b_sel == float(r), i2_sc[r:r + 1, :], e2)
        p = jnp.exp(top - jnp.max(top, axis=0, keepdims=True))
        gate = p / jnp.sum(p, axis=0, keepdims=True)
        row = pl.multiple_of(h * PEER_TOPK, PEER_TOPK)
        e1t_sc[pl.ds(row, PEER_TOPK), :] = e1
        e2t_sc[pl.ds(row, PEER_TOPK), :] = e2
        gt_sc[pl.ds(row, PEER_TOPK), :] = gate
        return carry

    lax.fori_loop(0, PEER_HEADS, head, 0)
    e1_ref[...] = e1t_sc[...].T
    e2_ref[...] = e2t_sc[...].T
    g_ref[...] = gt_sc[...].T


def _peer_topk(h2, wq, sk):
    t, d = h2.shape
    tm = min(512, t)
    nsel = PEER_HEADS * PEER_TOPK
    out = jax.ShapeDtypeStruct((t, nsel), F32)
    spec = pl.BlockSpec((tm, nsel), lambda i: (i, 0))
    small = lambda: pltpu.VMEM((PEER_TOPK, tm), F32)
    big = lambda: pltpu.VMEM((nsel, tm), F32)
    return pl.pallas_call(
        _peer_topk_kernel,
        out_shape=(out, out, out),
        grid=(t // tm,),
        in_specs=[pl.BlockSpec((tm, d), lambda i: (i, 0)),
                  pl.BlockSpec(wq.shape, lambda i: (0, 0)),
                  pl.BlockSpec(sk.shape, lambda i: (0, 0, 0))],
        out_specs=(spec, spec, spec),
        scratch_shapes=[pltpu.VMEM((2 * PEER_HEADS, tm, PEER_HALF), BF16),
                        small(), small(), small(), small(), small(), small(), big(), big(), big()],
        compiler_params=_cparams(1),
        name="peer_topk",
    )(h2, wq, sk)


def _peer_gates_kernel(e1_ref, e2_ref, g_ref, o_ref, g_sc, *, pitch):
    tm = e1_ref.shape[0]
    row_idx = lax.broadcasted_iota(jnp.int32, (PEER_N_KEYS, e1_ref.shape[1]), 0).astype(F32)

    def token(t, carry):
        r1 = e1_ref[pl.ds(t, 1), :]
        r2 = e2_ref[pl.ds(t, 1), :]
        gt = g_ref[pl.ds(t, 1), :]
        sel_i = jnp.where(row_idx == r1, 1.0, 0.0).astype(BF16)
        sel_j = jnp.where(row_idx == r2, gt, 0.0).astype(BF16)
        gm = lax.dot_general(sel_i, sel_j, _NT, preferred_element_type=F32)
        g_sc[pl.ds(t, PEER_N_KEYS, stride=pitch), :] = gm
        return carry

    lax.fori_loop(0, tm, token, 0, unroll=32)

    def emit(i, carry):
        start = pl.multiple_of(i * pitch, 8)
        o_ref[i] = g_sc[pl.ds(start, tm), :].astype(BF16)
        return carry

    lax.fori_loop(0, PEER_N_KEYS, emit, 0)


def _peer_gates(e1, e2, gate):
    t, nsel = e1.shape
    tm = min(256, t)
    pitch = tm + 8
    spec = pl.BlockSpec((tm, nsel), lambda i: (i, 0))
    return pl.pallas_call(
        functools.partial(_peer_gates_kernel, pitch=pitch),
        out_shape=jax.ShapeDtypeStruct((PEER_N_KEYS, t, PEER_N_KEYS), BF16),
        grid=(t // tm,),
        in_specs=[spec, spec, spec],
        out_specs=pl.BlockSpec((PEER_N_KEYS, tm, PEER_N_KEYS), lambda i: (0, i, 0)),
        scratch_shapes=[pltpu.VMEM((PEER_N_KEYS * pitch, PEER_N_KEYS), F32)],
        compiler_params=_cparams(1),
        name="peer_gate_matrix",
    )(e1, e2, gate)


def _peer_dense_kernel(h_ref, u_ref, v_ref, g_ref, x1_ref, mod_ref, o_ref, acc_sc):
    c = pl.program_id(1)

    @pl.when(c == 0)
    def _():
        acc_sc[...] = jnp.zeros_like(acc_sc)

    a = lax.dot_general(h_ref[...], u_ref[...], _NT, preferred_element_type=F32)
    act = 0.5 * a * (1.0 + lax.erf(a * (1.0 / math.sqrt(2.0))))
    ws = [(act[:, ib * LANES:(ib + 1) * LANES] * g_ref[ib].astype(F32)).astype(BF16)
          for ib in range(g_ref.shape[0])]
    w = jnp.concatenate(ws, axis=1)
    acc_sc[...] += jnp.dot(w, v_ref[...], preferred_element_type=F32)

    @pl.when(c == pl.num_programs(1) - 1)
    def _():
        o_ref[...] = x1_ref[...] + mod_ref[0][5:6] * acc_sc[...]


def _peer_dense(h2, u, v, gmat, x1, mod3, seq):
    t, d = h2.shape
    n_exp = u.shape[0]
    tm = min(1024, seq)
    spt = seq // tm
    tn = 1024
    return pl.pallas_call(
        _peer_dense_kernel,
        out_shape=jax.ShapeDtypeStruct((t, d), F32),
        grid=(t // tm, n_exp // tn),
        in_specs=[pl.BlockSpec((tm, d), lambda i, c: (i, 0)),
                  pl.BlockSpec((tn, d), lambda i, c: (c, 0)),
                  pl.BlockSpec((tn, d), lambda i, c: (c, 0)),
                  pl.BlockSpec((tn // PEER_N_KEYS, tm, PEER_N_KEYS), lambda i, c: (c, i, 0)),
                  pl.BlockSpec((tm, d), lambda i, c: (i, 0)),
                  pl.BlockSpec((1, 6, d), lambda i, c: (i // spt, 0, 0))],
        out_specs=pl.BlockSpec((tm, d), lambda i, c: (i, 0)),
        scratch_shapes=[pltpu.VMEM((tm, d), F32)],
        compiler_params=_cparams(2),
        name="peer_dense",
    )(h2, u, v, gmat, x1, mod3)


def _layer(x, c, cos, sin, lam_init, w_ada, b_ada, norm1_g, w_in, b_gate, qn_a, kn_a, w_proj_a,
           qn_b, kn_b, lam_q1, lam_k1, lam_q2, lam_k2, subln_g, w_proj_b, w_out, norm2_g,
           w_query, sub_keys, expert_u, expert_v):
    bsz, seq, d = x.shape
    t = bsz * seq
    x2 = x.reshape(t, d)
    mod3 = _ada(c, w_ada, b_ada).reshape(bsz, 6, d)
    gmat = _group_sum_matrix()
    scale = HEAD_DIM ** -0.5

    dil_w = len(DIL_GROUPS) * DIL_GROUP_WIDTH
    diff_w = DIFF_HEADS * 2 * HEAD_DIM
    a_cols = 3 * dil_w
    w_in_bf = w_in.astype(BF16)
    ones = lambda n: jnp.ones((n,), F32)
    gain_a = jnp.concatenate([jnp.tile(qn_a, dil_w // HEAD_DIM) * scale, jnp.tile(kn_a, dil_w // HEAD_DIM),
                              ones(dil_w)]).reshape(1, a_cols)
    qkv_groups = _inproj_a(x2, mod3, norm1_g.reshape(1, d), w_in_bf[:, :a_cols], gain_a, cos, sin, gmat, bsz, seq)

    n_b = w_in.shape[1] - a_cols
    gain_b = jnp.concatenate([jnp.tile(qn_b, diff_w // HEAD_DIM) * (scale * LOG2E),
                              jnp.tile(kn_b, diff_w // HEAD_DIM), ones(n_b - 2 * diff_w)]).reshape(1, n_b)
    bias_b = jnp.concatenate([jnp.zeros((3 * diff_w,), F32), b_gate]).reshape(1, n_b)
    qk_b, vt_b, gates = _inproj_b(x2, mod3, norm1_g.reshape(1, d), w_in_bf[:, a_cols:], gain_b, bias_b,
                                  cos, sin, gmat, seq, diff_w, diff_w)

    outs, lses = [], []
    for qkv in qkv_groups:
        o, l = _dilated_attn(qkv)
        outs.append(o)
        lses.append(l)

    lam_vecs = [v.reshape(1, HEAD_DIM) for v in (lam_q1, lam_k1, lam_q2, lam_k2)]
    score_cap = (HEAD_DIM * scale * LOG2E * 1.01) * jnp.max(jnp.abs(qn_b)) * jnp.max(jnp.abs(kn_b))
    score_bounded = (score_cap <= DIFF_SCORE_BOUND).astype(jnp.int32).reshape(1)
    ob = _diff_attn(qk_b, vt_b, score_bounded, lam_vecs, subln_g.reshape(1, 2 * HEAD_DIM), bsz, seq, lam_init)

    x1, h2 = _merge(outs, lses, ob, gates, x2, mod3, norm2_g.reshape(1, d), w_proj_a.astype(BF16),
                    w_proj_b.astype(BF16), w_out.astype(BF16), bsz, seq)

    sk = sub_keys.astype(BF16).reshape(PEER_HEADS * 2, PEER_N_KEYS, PEER_HALF)
    e1, e2, gate = _peer_topk(h2, w_query.astype(BF16), sk)
    gdense = _peer_gates(e1, e2, gate)
    out = _peer_dense(h2, expert_u.astype(BF16), expert_v.astype(BF16), gdense, x1, mod3, seq)
    return out.reshape(bsz, seq, d)


def kernel(x, c, positions, w_ada, b_ada, norm1_g, w_in, b_gate, qn_a, kn_a, w_proj_a, qn_b, kn_b,
           lam_q1, lam_k1, lam_q2, lam_k2, subln_g, w_proj_b, w_out, norm2_g, w_query, sub_keys,
           expert_u, expert_v):
    cos, sin = _rope_tables(positions)
    for l in range(w_ada.shape[0]):
        lam_init = 0.8 - 0.6 * math.exp(-0.3 * l)
        x = _layer(x, c, cos, sin, lam_init, w_ada[l], b_ada[l], norm1_g[l], w_in[l], b_gate[l],
                   qn_a[l], kn_a[l], w_proj_a[l], qn_b[l], kn_b[l], lam_q1[l], lam_k1[l], lam_q2[l],
                   lam_k2[l], subln_g[l], w_proj_b[l], w_out[l], norm2_g[l], w_query[l], sub_keys[l],
                   expert_u[l], expert_v[l])
    return x
```
